```python
import jax, jax.numpy as jnp
from jax import lax
import numpy as np

D_MODEL = 1024
BATCH = 2
SEQ = 8192
DEPTH = 1

SGU_WIDTH = D_MODEL
SGU_GROUPS = 8
SGU_GROUP_DIM = SGU_WIDTH // SGU_GROUPS
CHUNK = 128
HEAD_DIM = 64
N_Q_HEADS = D_MODEL // HEAD_DIM
N_KV_HEADS = N_Q_HEADS // 4
Q_PER_KV = N_Q_HEADS // N_KV_HEADS
WINDOW = 128
BLOCK = 128
ROPE_DIM = HEAD_DIM // 4
ROPE_THETA = 500000.0
ATTN_WIDTH = N_Q_HEADS * HEAD_DIM
KV_WIDTH = N_KV_HEADS * HEAD_DIM
D_FF = 4 * D_MODEL
EPS = 1e-6

IN_SPLITS = (SGU_WIDTH, SGU_WIDTH, ATTN_WIDTH, KV_WIDTH, KV_WIDTH, D_MODEL, D_MODEL)
IN_WIDTH = sum(IN_SPLITS)

kernel_name = "hybrid_gated_sgu_swa_sink_block"


def rms_norm(x, g):
    xf = x.astype(jnp.float32)
    y = xf * lax.rsqrt(jnp.mean(xf * xf, axis=-1, keepdims=True) + EPS)
    return (y * g.astype(jnp.float32)).astype(x.dtype)


def layer_norm(x, g, b):
    xf = x.astype(jnp.float32)
    mu = jnp.mean(xf, axis=-1, keepdims=True)
    var = jnp.mean(jnp.square(xf - mu), axis=-1, keepdims=True)
    y = (xf - mu) * lax.rsqrt(var + EPS)
    return (y * g.astype(jnp.float32) + b.astype(jnp.float32)).astype(x.dtype)


def partial_rotary(x, positions):
    inv_freq = ROPE_THETA ** (-jnp.arange(0, ROPE_DIM, 2, dtype=jnp.float32) / ROPE_DIM)
    ang = positions.astype(jnp.float32)[..., None] * inv_freq
    cos = jnp.cos(ang)[:, :, None, :]
    sin = jnp.sin(ang)[:, :, None, :]
    xr = x[..., :ROPE_DIM].astype(jnp.float32)
    x1, x2 = xr[..., : ROPE_DIM // 2], xr[..., ROPE_DIM // 2:]
    rot = jnp.concatenate([x1 * cos - x2 * sin, x2 * cos + x1 * sin], axis=-1).astype(x.dtype)
    return jnp.concatenate([rot, x[..., ROPE_DIM:]], axis=-1)


def chunked_spatial_gating(u, v, ln_g, ln_b, w_s, b_s):
    B, S, _ = v.shape
    n_chunks = S // CHUNK
    vn = layer_norm(v, ln_g, ln_b)
    vc = vn.reshape(B, n_chunks, CHUNK, SGU_GROUPS, SGU_GROUP_DIM)
    causal = jnp.tril(jnp.ones((CHUNK, CHUNK), dtype=bool))
    w = jnp.where(causal[None], w_s, jnp.zeros_like(w_s))
    mixed = jnp.einsum('gts,bcsge->bctge', w, vc) + b_s.T[None, None, :, :, None]
    return u * mixed.reshape(B, S, SGU_WIDTH)


def sliding_window_sink_attention(q, k, v, sinks):
    B, S = q.shape[0], q.shape[1]
    n_blocks = S // BLOCK
    qb = q.reshape(B, n_blocks, BLOCK, N_KV_HEADS, Q_PER_KV, HEAD_DIM)
    kb = k.reshape(B, n_blocks, BLOCK, N_KV_HEADS, HEAD_DIM)
    vb = v.reshape(B, n_blocks, BLOCK, N_KV_HEADS, HEAD_DIM)
    pad = ((0, 0), (1, 0), (0, 0), (0, 0), (0, 0))
    k_band = jnp.concatenate([jnp.pad(kb, pad)[:, :-1], kb], axis=2)
    v_band = jnp.concatenate([jnp.pad(vb, pad)[:, :-1], vb], axis=2)
    scores = jnp.einsum('bnqgrd,bnkgd->bngrqk', qb, k_band).astype(jnp.float32) * (HEAD_DIM ** -0.5)
    blk = jnp.arange(n_blocks)[:, None]
    qpos = blk * BLOCK + jnp.arange(BLOCK)[None, :]
    kpos = (blk - 1) * BLOCK + jnp.arange(2 * BLOCK)[None, :]
    diff = qpos[:, :, None] - kpos[:, None, :]
    allowed = (diff >= 0) & (diff < WINDOW) & (kpos[:, None, :] >= 0)
    scores = jnp.where(allowed[None, :, None, None], scores, jnp.float32(-1e30))
    sink = sinks.astype(jnp.float32).reshape(N_KV_HEADS, Q_PER_KV)[None, None, :, :, None, None]
    m = jnp.maximum(jnp.max(scores, axis=-1, keepdims=True), sink)
    p = jnp.exp(scores - m)
    denom = jnp.sum(p, axis=-1, keepdims=True) + jnp.exp(sink - m)
    p = (p / denom).astype(v.dtype)
    out = jnp.einsum('bngrqk,bnkgd->bnqgrd', p, v_band)
    return out.reshape(B, S, ATTN_WIDTH)


def hybrid_layer(x, positions, w_in, ln_v_gain, ln_v_bias, w_spatial, b_spatial, sinks,
                 w_a, w_b, w_o, norm_mix_pre, norm_mix_post,
                 w_ff_in, w_ff_out, norm_ff_pre, norm_ff_post):
    B, S, _ = x.shape
    h = rms_norm(x, norm_mix_pre)
    proj = h @ w_in
    offs = np.cumsum(IN_SPLITS)[:-1].tolist()
    u, v_sgu, q, k, v_att, gate_a, gate_b = jnp.split(proj, offs, axis=-1)
    a = chunked_spatial_gating(jax.nn.gelu(u), jax.nn.gelu(v_sgu), ln_v_gain, ln_v_bias, w_spatial, b_spatial)
    q = partial_rotary(q.reshape(B, S, N_Q_HEADS, HEAD_DIM), positions)
    k = partial_rotary(k.reshape(B, S, N_KV_HEADS, HEAD_DIM), positions)
    v_att = v_att.reshape(B, S, N_KV_HEADS, HEAD_DIM)
    att = sliding_window_sink_attention(q, k, v_att, sinks)
    merged = jax.nn.sigmoid(gate_a) * (a @ w_a) + jax.nn.sigmoid(gate_b) * (att @ w_b)
    x = x + rms_norm(merged @ w_o, norm_mix_post)
    hf = rms_norm(x, norm_ff_pre)
    ff = jnp.square(jax.nn.relu(hf @ w_ff_in)) @ w_ff_out
    return x + rms_norm(ff, norm_ff_post)


def setup_inputs(seed: int = 0) -> dict:
    key = jax.random.key(seed)
    ks = jax.random.split(key, 20)
    f32 = jnp.float32
    def nrm(k, shape, scale):
        return jax.random.normal(k, shape, dtype=f32) * scale
    def gain(k, shape):
        return 1.0 + 0.05 * jax.random.normal(k, shape, dtype=f32)
    x = jax.random.normal(ks[0], (BATCH, SEQ, D_MODEL), dtype=f32)
    offset = jax.random.randint(ks[1], (BATCH, 1), 0, 4096, dtype=jnp.int32)
    positions = offset + jnp.arange(SEQ, dtype=jnp.int32)[None, :]
    return {
        "x": x,
        "positions": positions,
        "w_in": nrm(ks[2], (DEPTH, D_MODEL, IN_WIDTH), D_MODEL ** -0.5),
        "ln_v_gain": gain(ks[3], (DEPTH, SGU_WIDTH)),
        "ln_v_bias": nrm(ks[4], (DEPTH, SGU_WIDTH), 0.02),
        "w_spatial": nrm(ks[5], (DEPTH, SGU_GROUPS, CHUNK, CHUNK), CHUNK ** -0.5),
        "b_spatial": gain(ks[6], (DEPTH, SGU_GROUPS, CHUNK)),
        "sinks": nrm(ks[7], (DEPTH, N_Q_HEADS), 0.5),
        "w_a": nrm(ks[8], (DEPTH, SGU_WIDTH, D_MODEL), SGU_WIDTH ** -0.5),
        "w_b": nrm(ks[9], (DEPTH, ATTN_WIDTH, D_MODEL), ATTN_WIDTH ** -0.5),
        "w_o": nrm(ks[10], (DEPTH, D_MODEL, D_MODEL), D_MODEL ** -0.5),
        "norm_mix_pre": gain(ks[11], (DEPTH, D_MODEL)),
        "norm_mix_post": gain(ks[12], (DEPTH, D_MODEL)),
        "w_ff_in": nrm(ks[13], (DEPTH, D_MODEL, D_FF), D_MODEL ** -0.5),
        "w_ff_out": nrm(ks[14], (DEPTH, D_FF, D_MODEL), D_FF ** -0.5),
        "norm_ff_pre": gain(ks[15], (DEPTH, D_MODEL)),
        "norm_ff_post": gain(ks[16], (DEPTH, D_MODEL)),
    }


def reference(x, positions, w_in, ln_v_gain, ln_v_bias, w_spatial, b_spatial, sinks,
              w_a, w_b, w_o, norm_mix_pre, norm_mix_post,
              w_ff_in, w_ff_out, norm_ff_pre, norm_ff_post):
    for l in range(DEPTH):
        x = hybrid_layer(x, positions, w_in[l], ln_v_gain[l], ln_v_bias[l], w_spatial[l], b_spatial[l],
                         sinks[l], w_a[l], w_b[l], w_o[l], norm_mix_pre[l], norm_mix_post[l],
                         w_ff_in[l], w_ff_out[l], norm_ff_pre[l], norm_ff_post[l])
    return x
```

```python
import functools

import jax
import jax.numpy as jnp
import numpy as np
from jax import lax
from jax.experimental import pallas as pl
from jax.experimental.pallas import tpu as pltpu

D_MODEL = 1024
SGU_GROUPS = 8
CHUNK = 128
HEAD_DIM = 64
N_Q_HEADS = 16
N_KV_HEADS = 4
KV_WIDTH = N_KV_HEADS * HEAD_DIM
ROPE_DIM = 16
ROPE_THETA = 500000.0
D_FF = 4 * D_MODEL
EPS = 1e-6
LANES = 128
NEG_INF = -1e30

_OFF_U = 0
_OFF_V = _OFF_U + D_MODEL
_OFF_Q = _OFF_V + D_MODEL
_OFF_K = _OFF_Q + D_MODEL
_OFF_VA = _OFF_K + KV_WIDTH
_OFF_GA = _OFF_VA + KV_WIDTH
_OFF_GB = _OFF_GA + D_MODEL
IN_WIDTH = _OFF_GB + D_MODEL

VMEM_LIMIT_BYTES = 56 * 1024 * 1024
PROJ_ROWS = 256
OUT_ROWS = 256
FF_CHUNK = 1024

BF16 = jnp.bfloat16
F32 = jnp.float32


def _rms(x, gain):
    return x * lax.rsqrt(jnp.mean(x * x, axis=-1, keepdims=True) + EPS) * gain


def _dot(a, b):
    return jnp.dot(a, b, preferred_element_type=F32)


def _sigmoid(x):
    return 1.0 / (1.0 + jnp.exp(-x))


def _proj_kernel(x_ref, pos_ref, invf_ref, gpre_ref, w_ref, lng_ref, lnb_ref,
                 u_ref, vn_ref, q_ref, k_ref, va_ref, ga_ref, gb_ref):
    rows = x_ref.shape[0]
    h = _rms(x_ref[...], gpre_ref[...]).astype(BF16)

    def proj(lo, width):
        return _dot(h, w_ref[:, lo:lo + width])

    u_ref[...] = jax.nn.gelu(proj(_OFF_U, D_MODEL)).astype(BF16)

    v = jax.nn.gelu(proj(_OFF_V, D_MODEL))
    mu = jnp.mean(v, axis=-1, keepdims=True)
    vc = v - mu
    var = jnp.mean(vc * vc, axis=-1, keepdims=True)
    vn_ref[...] = (vc * lax.rsqrt(var + EPS) * lng_ref[...] + lnb_ref[...]).astype(BF16)

    va_ref[...] = proj(_OFF_VA, KV_WIDTH).astype(BF16)
    ga_ref[...] = _sigmoid(proj(_OFF_GA, D_MODEL)).astype(BF16)
    gb_ref[...] = _sigmoid(proj(_OFF_GB, D_MODEL)).astype(BF16)

    lane = lax.broadcasted_iota(jnp.int32, (CHUNK, LANES), 1)
    d = lane & (HEAD_DIM - 1)
    half = ROPE_DIM // 2
    q = proj(_OFF_Q, D_MODEL)
    k = proj(_OFF_K, KV_WIDTH)
    for j in range(rows // CHUNK):
        pos_row = pos_ref[0, j:j + 1, :].astype(F32)
        pos_col = jnp.broadcast_to(pos_row, (CHUNK, LANES)).T
        ang = pos_col * invf_ref[...]
        cos = jnp.cos(ang)
        sin = jnp.sin(ang)
        c_tab = jnp.where(d < ROPE_DIM, cos, 1.0)
        s_next = jnp.where(d < half, -sin, 0.0)
        s_prev = jnp.where((d >= half) & (d < ROPE_DIM), sin, 0.0)
        r0 = j * CHUNK

        def rot(x):
            return (x * c_tab
                    + pltpu.roll(x, LANES - half, axis=1) * s_next
                    + pltpu.roll(x, half, axis=1) * s_prev)

        for c in range(D_MODEL // LANES):
            xq = q[r0:r0 + CHUNK, c * LANES:(c + 1) * LANES]
            q_ref[r0:r0 + CHUNK, c * LANES:(c + 1) * LANES] = (
                rot(xq) * (HEAD_DIM ** -0.5)).astype(BF16)
        for c in range(KV_WIDTH // LANES):
            xk = k[r0:r0 + CHUNK, c * LANES:(c + 1) * LANES]
            k_ref[r0:r0 + CHUNK, c * LANES:(c + 1) * LANES] = rot(xk).astype(BF16)


def _projection(x2, pos3, invf, gpre, w_in, lng, lnb):
    tokens = x2.shape[0]
    rows = PROJ_ROWS
    grid = (tokens // rows,)
    row_spec = lambda width: pl.BlockSpec((rows, width), lambda i: (i, 0))
    full = lambda shape: pl.BlockSpec(shape, lambda i: (0,) * len(shape))
    out_widths = (D_MODEL, D_MODEL, D_MODEL, KV_WIDTH, KV_WIDTH, D_MODEL, D_MODEL)
    return pl.pallas_call(
        _proj_kernel,
        grid=grid,
        in_specs=[
            row_spec(D_MODEL),
            pl.BlockSpec((1, rows // CHUNK, LANES), lambda i: (i, 0, 0)),
            full((1, LANES)),
            full((1, D_MODEL)),
            full((D_MODEL, IN_WIDTH)),
            full((1, D_MODEL)),
            full((1, D_MODEL)),
        ],
        out_specs=[row_spec(w) for w in out_widths],
        out_shape=[jax.ShapeDtypeStruct((tokens, w), BF16) for w in out_widths],
        compiler_params=pltpu.CompilerParams(
            dimension_semantics=("arbitrary",), vmem_limit_bytes=VMEM_LIMIT_BYTES),
        name="proj",
    )(x2, pos3, invf, gpre, w_in, lng, lnb)


def _mix_kernel(blocks_per_seq, sinks_ref, u_ref, vn_ref, q_ref, k_ref, va_ref,
                ws_ref, bs_ref, a_ref, att_ref, kprev_ref, vprev_ref):
    i = pl.program_id(0)

    @pl.when(i == 0)
    def _():
        kprev_ref[...] = jnp.zeros_like(kprev_ref)
        vprev_ref[...] = jnp.zeros_like(vprev_ref)

    row = lax.broadcasted_iota(jnp.int32, (CHUNK, CHUNK), 0)
    col = lax.broadcasted_iota(jnp.int32, (CHUNK, CHUNK), 1)
    causal = col <= row

    for g in range(SGU_GROUPS):
        w = jnp.where(causal, ws_ref[g], 0.0).astype(BF16)
        sl = slice(g * LANES, (g + 1) * LANES)
        mixed = _dot(w, vn_ref[:, sl]) + bs_ref[:, g:g + 1]
        a_ref[:, sl] = (u_ref[:, sl].astype(F32) * mixed).astype(BF16)

    first = (i % blocks_per_seq) == 0
    prev_bias = jnp.where(first, NEG_INF, 0.0).astype(F32)
    low = col < HEAD_DIM
    causal2 = jnp.concatenate([causal, causal], axis=0)
    row2 = lax.broadcasted_iota(jnp.int32, (2 * CHUNK, 1), 0)

    kdup, vdup = [], []
    for g in range(N_KV_HEADS):
        sl = slice((g // 2) * LANES, (g // 2 + 1) * LANES)
        keep_low = low if g % 2 == 0 else ~low
        for src, dst in ((k_ref, kdup), (va_ref, vdup)):
            t = src[:, sl].astype(F32)
            dst.append(jnp.where(keep_low, t, pltpu.roll(t, HEAD_DIM, axis=1)).astype(BF16))

    for j in range(N_Q_HEADS // 2):
        g = j // 2
        qp = q_ref[:, j * LANES:(j + 1) * LANES]
        zero = jnp.zeros_like(qp)
        q2 = jnp.concatenate([jnp.where(low, qp, zero), jnp.where(low, zero, qp)], axis=0)
        kband = jnp.concatenate([kprev_ref[g], kdup[g]], axis=0)
        s2 = lax.dot_general(q2, kband, (((1,), (1,)), ((), ())),
                             preferred_element_type=F32)
        s = jnp.where(causal2, s2[:, CHUNK:], s2[:, :CHUNK] + prev_bias)
        sink = jnp.where(row2 < CHUNK, sinks_ref[2 * j], sinks_ref[2 * j + 1])
        m = jnp.maximum(jnp.max(s, axis=-1, keepdims=True), sink)
        p = jnp.exp(s - m)
        denom = jnp.sum(p, axis=-1, keepdims=True) + jnp.exp(sink - m)
        p = p * (1.0 / denom)
        pz = jnp.zeros_like(p)
        pband = jnp.concatenate([jnp.where(causal2, pz, p), jnp.where(causal2, p, pz)],
                                axis=1).astype(BF16)
        vband = jnp.concatenate([vprev_ref[g], vdup[g]], axis=0)
        o2 = _dot(pband, vband)
        att_ref[:, j * LANES:(j + 1) * LANES] = jnp.where(
            low, o2[:CHUNK], o2[CHUNK:]).astype(BF16)

    for g in range(N_KV_HEADS):
        kprev_ref[g] = kdup[g]
        vprev_ref[g] = vdup[g]


def _mixing(sinks, u, vn, q, k, va, w_spatial, b_spatial_t, blocks_per_seq):
    tokens = u.shape[0]
    grid = (tokens // CHUNK,)
    blk = lambda width: pl.BlockSpec((CHUNK, width), lambda i: (i, 0))
    return pl.pallas_call(
        functools.partial(_mix_kernel, blocks_per_seq),
        grid=grid,
        in_specs=[
            pl.BlockSpec(memory_space=pltpu.SMEM),
            blk(D_MODEL), blk(D_MODEL), blk(D_MODEL), blk(KV_WIDTH), blk(KV_WIDTH),
            pl.BlockSpec((SGU_GROUPS, CHUNK, CHUNK), lambda i: (0, 0, 0)),
            pl.BlockSpec((CHUNK, SGU_GROUPS), lambda i: (0, 0)),
        ],
        out_specs=[blk(D_MODEL), blk(D_MODEL)],
        out_shape=[jax.ShapeDtypeStruct((tokens, D_MODEL), BF16)] * 2,
        scratch_shapes=[pltpu.VMEM((N_KV_HEADS, CHUNK, LANES), BF16)] * 2,
        compiler_params=pltpu.CompilerParams(
            dimension_semantics=("arbitrary",), vmem_limit_bytes=VMEM_LIMIT_BYTES),
        name="mix",
    )(sinks, u, vn, q, k, va, w_spatial, b_spatial_t)


def _out_kernel(x_ref, a_ref, att_ref, ga_ref, gb_ref, wa_ref, wb_ref, wo_ref,
                gpost_ref, wf1_ref, wf2_ref, gffpre_ref, gffpost_ref, o_ref, h1_ref):
    merged = (ga_ref[...].astype(F32) * _dot(a_ref[...], wa_ref[...])
              + gb_ref[...].astype(F32) * _dot(att_ref[...], wb_ref[...]))
    mix = _dot(merged.astype(BF16), wo_ref[...])
    x1 = x_ref[...] + _rms(mix, gpost_ref[...])
    hf = _rms(x1, gffpre_ref[...]).astype(BF16)
    for c in range(D_FF // FF_CHUNK):
        sl = slice(c * FF_CHUNK, (c + 1) * FF_CHUNK)
        h1 = jnp.maximum(_dot(hf, wf1_ref[:, sl]), 0.0)
        h1_ref[:, sl] = (h1 * h1).astype(BF16)
    ff = _dot(h1_ref[...], wf2_ref[...])
    o_ref[...] = x1 + _rms(ff, gffpost_ref[...])


def _output(x2, a, att, ga, gb, wa, wb, wo, gpost, wf1, wf2, gffpre, gffpost):
    tokens = x2.shape[0]
    rows = OUT_ROWS
    grid = (tokens // rows,)
    row_spec = pl.BlockSpec((rows, D_MODEL), lambda i: (i, 0))
    const = lambda shape: pl.BlockSpec(shape, lambda i: (0,) * len(shape),
                                       pipeline_mode=pl.Buffered(1))
    return pl.pallas_call(
        _out_kernel,
        grid=grid,
        in_specs=[
            row_spec, row_spec, row_spec, row_spec, row_spec,
            const((D_MODEL, D_MODEL)), const((D_MODEL, D_MODEL)), const((D_MODEL, D_MODEL)),
            const((1, D_MODEL)),
            const((D_MODEL, D_FF)), const((D_FF, D_MODEL)),
            const((1, D_MODEL)), const((1, D_MODEL)),
        ],
        out_specs=row_spec,
        out_shape=jax.ShapeDtypeStruct((tokens, D_MODEL), F32),
        scratch_shapes=[pltpu.VMEM((rows, D_FF), BF16)],
        compiler_params=pltpu.CompilerParams(
            dimension_semantics=("arbitrary",), vmem_limit_bytes=VMEM_LIMIT_BYTES),
        name="out",
    )(x2, a, att, ga, gb, wa, wb, wo, gpost, wf1, wf2, gffpre, gffpost)


def _rope_inv_freq_lanes():
    inv_freq = ROPE_THETA ** (-jnp.arange(0, ROPE_DIM, 2, dtype=F32) / ROPE_DIM)
    d = np.arange(LANES) % HEAD_DIM
    idx = np.where(d < ROPE_DIM, d % (ROPE_DIM // 2), 0)
    return jnp.where(jnp.asarray(d < ROPE_DIM), inv_freq[idx], 0.0).reshape(1, LANES)


def kernel(x, positions, w_in, ln_v_gain, ln_v_bias, w_spatial, b_spatial, sinks, w_a, w_b, w_o, norm_mix_pre, norm_mix_post, w_ff_in, w_ff_out, norm_ff_pre, norm_ff_post):
    batch, seq, _ = x.shape
    depth = w_in.shape[0]
    tokens = batch * seq
    row = lambda v: v.reshape(1, -1)
    invf = _rope_inv_freq_lanes()
    pos3 = positions.reshape(tokens // PROJ_ROWS, PROJ_ROWS // CHUNK, LANES)
    x2 = x.reshape(tokens, D_MODEL)
    for l in range(depth):
        u, vn, q, k, va, ga, gb = _projection(
            x2, pos3, invf, row(norm_mix_pre[l]), w_in[l].astype(BF16),
            row(ln_v_gain[l]), row(ln_v_bias[l]))
        a, att = _mixing(sinks[l], u, vn, q, k, va, w_spatial[l], b_spatial[l].T,
                         seq // CHUNK)
        x2 = _output(x2, a, att, ga, gb, w_a[l].astype(BF16), w_b[l].astype(BF16),
                     w_o[l].astype(BF16), row(norm_mix_post[l]),
                     w_ff_in[l].astype(BF16), w_ff_out[l].astype(BF16),
                     row(norm_ff_pre[l]), row(norm_ff_post[l]))
    return x2.reshape(batch, seq, D_MODEL)
```

```python
import functools

import jax
import jax.numpy as jnp
import numpy as np
from jax import lax
from jax.experimental import pallas as pl
from jax.experimental.pallas import tpu as pltpu

D_MODEL = 1024
SGU_GROUPS = 8
CHUNK = 128
HEAD_DIM = 64
N_Q_HEADS = 16
N_KV_HEADS = 4
Q_PER_KV = N_Q_HEADS // N_KV_HEADS
KV_WIDTH = N_KV_HEADS * HEAD_DIM
ROPE_DIM = 16
ROPE_HALF = ROPE_DIM // 2
ROPE_THETA = 500000.0
D_FF = 4 * D_MODEL
EPS = 1e-6
LANES = 128
SUBLANES = 8
NEG_INF = -1e30
KV_DUP_WIDTH = N_KV_HEADS * LANES

_OFF_U = 0
_OFF_V = _OFF_U + D_MODEL
_OFF_Q = _OFF_V + D_MODEL
_OFF_K = _OFF_Q + D_MODEL
_OFF_VA = _OFF_K + KV_WIDTH
_OFF_GA = _OFF_VA + KV_WIDTH
_OFF_GB = _OFF_GA + D_MODEL
IN_WIDTH = _OFF_GB + D_MODEL

VMEM_LIMIT_BYTES = 56 * 1024 * 1024
PROJ_ROWS = 256
MIX_BLOCKS = 4
OUT_ROWS = 256
FF_CHUNK = 1024

BF16 = jnp.bfloat16
F32 = jnp.float32


def _rms(x, gain):
    return x * lax.rsqrt(jnp.mean(x * x, axis=-1, keepdims=True) + EPS) * gain


def _dot(a, b):
    return jnp.dot(a, b, preferred_element_type=F32)


def _sigmoid(x):
    return 1.0 / (1.0 + jnp.exp(-x))


def _low_half_lanes(rows):
    return lax.broadcasted_iota(jnp.int32, (rows, LANES), 1) < HEAD_DIM


def _rope_tables(pos_row, invf):
    ang = invf * pos_row
    cos = jnp.cos(ang)
    sin = jnp.sin(ang)
    pad = HEAD_DIM - ROPE_DIM
    ones = jnp.ones((pad, LANES), F32)
    zeros = jnp.zeros((pad, LANES), F32)
    cos_lt = jnp.concatenate([cos, cos, ones] * (LANES // HEAD_DIM), axis=0)
    sin_lt = jnp.concatenate([-sin, sin, zeros] * (LANES // HEAD_DIM), axis=0)
    return cos_lt.T, sin_lt.T


def _proj_kernel(x_ref, pos_ref, invf_ref, gpre_ref, w_ref, lng_ref, lnb_ref,
                 u_ref, vn_ref, q_ref, kd_ref, vd_ref, ga_ref, gb_ref):
    rows = x_ref.shape[0]
    h = _rms(x_ref[...], gpre_ref[...]).astype(BF16)

    def proj(lo, width):
        return _dot(h, w_ref[:, lo:lo + width])

    u_ref[...] = jax.nn.gelu(proj(_OFF_U, D_MODEL)).astype(BF16)

    v = jax.nn.gelu(proj(_OFF_V, D_MODEL))
    mu = jnp.mean(v, axis=-1, keepdims=True)
    vc = v - mu
    var = jnp.mean(vc * vc, axis=-1, keepdims=True)
    vn_ref[...] = (vc * lax.rsqrt(var + EPS) * lng_ref[...] + lnb_ref[...]).astype(BF16)

    ga_ref[...] = _sigmoid(proj(_OFF_GA, D_MODEL)).astype(BF16)
    gb_ref[...] = _sigmoid(proj(_OFF_GB, D_MODEL)).astype(BF16)

    q = proj(_OFF_Q, D_MODEL)
    k = proj(_OFF_K, KV_WIDTH)
    va = proj(_OFF_VA, KV_WIDTH)
    low = _low_half_lanes(CHUNK)
    d = lax.broadcasted_iota(jnp.int32, (CHUNK, LANES), 1) & (HEAD_DIM - 1)
    pair_next = d < ROPE_HALF

    def store_dup(ref, r0, c, tile):
        rolled = pltpu.roll(tile, HEAD_DIM, axis=1)
        ref[r0:r0 + CHUNK, (2 * c) * LANES:(2 * c + 1) * LANES] = (
            jnp.where(low, tile, rolled).astype(BF16))
        ref[r0:r0 + CHUNK, (2 * c + 1) * LANES:(2 * c + 2) * LANES] = (
            jnp.where(low, rolled, tile).astype(BF16))

    for j in range(rows // CHUNK):
        r0 = j * CHUNK
        c_tab, s_tab = _rope_tables(pos_ref[0, j:j + 1, :].astype(F32), invf_ref[...])

        def rot(x, c_t, s_t):
            partner = jnp.where(pair_next, pltpu.roll(x, LANES - ROPE_HALF, axis=1),
                                pltpu.roll(x, ROPE_HALF, axis=1))
            return x * c_t + partner * s_t

        scale = HEAD_DIM ** -0.5
        cq, sq = c_tab * scale, s_tab * scale
        for c in range(D_MODEL // LANES):
            sl = slice(c * LANES, (c + 1) * LANES)
            q_ref[r0:r0 + CHUNK, sl] = rot(q[r0:r0 + CHUNK, sl], cq, sq).astype(BF16)
        for c in range(KV_WIDTH // LANES):
            sl = slice(c * LANES, (c + 1) * LANES)
            store_dup(kd_ref, r0, c, rot(k[r0:r0 + CHUNK, sl], c_tab, s_tab))
            store_dup(vd_ref, r0, c, va[r0:r0 + CHUNK, sl])


def _projection(x2, pos3, invf, gpre, w_in, lng, lnb):
    tokens = x2.shape[0]
    rows = PROJ_ROWS
    grid = (tokens // rows,)
    row_spec = lambda width: pl.BlockSpec((rows, width), lambda i: (i, 0))
    full = lambda shape: pl.BlockSpec(shape, lambda i: (0,) * len(shape))
    out_widths = (D_MODEL, D_MODEL, D_MODEL, KV_DUP_WIDTH, KV_DUP_WIDTH, D_MODEL, D_MODEL)
    return pl.pallas_call(
        _proj_kernel,
        grid=grid,
        in_specs=[
            row_spec(D_MODEL),
            pl.BlockSpec((1, rows // CHUNK, LANES), lambda i: (i, 0, 0)),
            full((SUBLANES, LANES)),
            full((1, D_MODEL)),
            full((D_MODEL, IN_WIDTH)),
            full((1, D_MODEL)),
            full((1, D_MODEL)),
        ],
        out_specs=[row_spec(w) for w in out_widths],
        out_shape=[jax.ShapeDtypeStruct((tokens, w), BF16) for w in out_widths],
        compiler_params=pltpu.CompilerParams(
            dimension_semantics=("arbitrary",), vmem_limit_bytes=VMEM_LIMIT_BYTES),
        name="proj",
    )(x2, pos3, invf, gpre, w_in, lng, lnb)


def _mix_kernel(blocks_per_seq, sinks_ref, u_ref, vn_ref, q_ref, kd_ref, vd_ref,
                kdp_ref, vdp_ref, ws_ref, bs_ref, a_ref, att_ref, wtril_ref):
    i = pl.program_id(0)
    row = lax.broadcasted_iota(jnp.int32, (CHUNK, CHUNK), 0)
    col = lax.broadcasted_iota(jnp.int32, (CHUNK, CHUNK), 1)
    causal = col <= row

    @pl.when(i == 0)
    def _():
        for g in range(SGU_GROUPS):
            wtril_ref[g] = jnp.where(causal, ws_ref[g], 0.0).astype(BF16)

    blk = lambda b: slice(b * CHUNK, (b + 1) * CHUNK)
    tile = lambda c: slice(c * LANES, (c + 1) * LANES)

    for g in range(SGU_GROUPS):
        bias = bs_ref[:, g:g + 1]
        for bp in range(MIX_BLOCKS // 2):
            b0, b1 = 2 * bp, 2 * bp + 1
            rhs = jnp.concatenate([vn_ref[blk(b0), tile(g)], vn_ref[blk(b1), tile(g)]], axis=1)
            mixed = _dot(wtril_ref[g], rhs) + bias
            a_ref[blk(b0), tile(g)] = (u_ref[blk(b0), tile(g)].astype(F32)
                                       * mixed[:, :LANES]).astype(BF16)
            a_ref[blk(b1), tile(g)] = (u_ref[blk(b1), tile(g)].astype(F32)
                                       * mixed[:, LANES:]).astype(BF16)

    first = ((i * MIX_BLOCKS) % blocks_per_seq) == 0
    prev_bias = jnp.where(first, NEG_INF, 0.0).astype(F32)
    low = col < HEAD_DIM
    ones = jnp.ones((2 * CHUNK, LANES), BF16)

    for b in range(MIX_BLOCKS):
        for g in range(N_KV_HEADS):
            if b == 0:
                k_prev, v_prev = kdp_ref[:, tile(g)], vdp_ref[:, tile(g)]
            else:
                k_prev, v_prev = kd_ref[blk(b - 1), tile(g)], vd_ref[blk(b - 1), tile(g)]
            kband = jnp.concatenate([k_prev, kd_ref[blk(b), tile(g)]], axis=0)
            vband = jnp.concatenate(
                [jnp.concatenate([v_prev, vd_ref[blk(b), tile(g)]], axis=0), ones], axis=1)
            q_tiles = []
            for t in range(Q_PER_KV // 2):
                qt = q_ref[blk(b), tile(2 * g + t)]
                zero = jnp.zeros_like(qt)
                q_tiles += [jnp.where(low, qt, zero), jnp.where(low, zero, qt)]
            q4 = jnp.concatenate(q_tiles, axis=0)
            s4 = lax.dot_general(q4, kband, (((1,), (1,)), ((), ())),
                                 preferred_element_type=F32)
            p_rows, sink_terms = [], []
            for hh in range(Q_PER_KV):
                s_prev = s4[blk(hh), :CHUNK]
                if b == 0:
                    s_prev = s_prev + prev_bias
                s = jnp.where(causal, s4[blk(hh), CHUNK:], s_prev)
                sink = sinks_ref[Q_PER_KV * g + hh]
                m = jnp.maximum(jnp.max(s, axis=-1, keepdims=True), sink)
                p = jnp.exp(s - m)
                pz = jnp.zeros_like(p)
                p_rows.append(jnp.concatenate(
                    [jnp.where(causal, pz, p), jnp.where(causal, p, pz)], axis=1).astype(BF16))
                sink_terms.append(jnp.exp(sink - m))
            pband = jnp.concatenate(p_rows, axis=0)
            o4 = _dot(pband, vband)
            outs = []
            for hh in range(Q_PER_KV):
                denom = o4[blk(hh), LANES:] + sink_terms[hh]
                outs.append(o4[blk(hh), :LANES] * (1.0 / denom))
            for t in range(Q_PER_KV // 2):
                att_ref[blk(b), tile(2 * g + t)] = jnp.where(
                    low, outs[2 * t], outs[2 * t + 1]).astype(BF16)


def _mixing(sinks, u, vn, q, kd, vd, w_spatial, b_spatial_t, blocks_per_seq):
    tokens = u.shape[0]
    rows = MIX_BLOCKS * CHUNK
    grid = (tokens // rows,)
    blk = lambda width: pl.BlockSpec((rows, width), lambda i: (i, 0))
    prev = pl.BlockSpec((CHUNK, KV_DUP_WIDTH),
                        lambda i: (jnp.maximum(i * MIX_BLOCKS - 1, 0), 0))
    return pl.pallas_call(
        functools.partial(_mix_kernel, blocks_per_seq),
        grid=grid,
        in_specs=[
            pl.BlockSpec(memory_space=pltpu.SMEM),
            blk(D_MODEL), blk(D_MODEL), blk(D_MODEL), blk(KV_DUP_WIDTH), blk(KV_DUP_WIDTH),
            prev, prev,
            pl.BlockSpec((SGU_GROUPS, CHUNK, CHUNK), lambda i: (0, 0, 0)),
            pl.BlockSpec((CHUNK, SGU_GROUPS), lambda i: (0, 0)),
        ],
        out_specs=[blk(D_MODEL), blk(D_MODEL)],
        out_shape=[jax.ShapeDtypeStruct((tokens, D_MODEL), BF16)] * 2,
        scratch_shapes=[pltpu.VMEM((SGU_GROUPS, CHUNK, CHUNK), BF16)],
        compiler_params=pltpu.CompilerParams(
            dimension_semantics=("arbitrary",), vmem_limit_bytes=VMEM_LIMIT_BYTES),
        name="mix",
    )(sinks, u, vn, q, kd, vd, kd, vd, w_spatial, b_spatial_t)


def _out_kernel(x_ref, a_ref, att_ref, ga_ref, gb_ref, wa_ref, wb_ref, wo_ref,
                gpost_ref, wf1_ref, wf2_ref, gffpre_ref, gffpost_ref, o_ref, h1_ref):
    merged = (ga_ref[...].astype(F32) * _dot(a_ref[...], wa_ref[...])
              + gb_ref[...].astype(F32) * _dot(att_ref[...], wb_ref[...]))
    mix = _dot(merged.astype(BF16), wo_ref[...])
    x1 = x_ref[...] + _rms(mix, gpost_ref[...])
    hf = _rms(x1, gffpre_ref[...]).astype(BF16)
    for c in range(D_FF // FF_CHUNK):
        sl = slice(c * FF_CHUNK, (c + 1) * FF_CHUNK)
        h1 = jnp.maximum(_dot(hf, wf1_ref[:, sl]), 0.0)
        h1_ref[:, sl] = (h1 * h1).astype(BF16)
    ff = _dot(h1_ref[...], wf2_ref[...])
    o_ref[...] = x1 + _rms(ff, gffpost_ref[...])


def _output(x2, a, att, ga, gb, wa, wb, wo, gpost, wf1, wf2, gffpre, gffpost):
    tokens = x2.shape[0]
    rows = OUT_ROWS
    grid = (tokens // rows,)
    row_spec = pl.BlockSpec((rows, D_MODEL), lambda i: (i, 0))
    const = lambda shape: pl.BlockSpec(shape, lambda i: (0,) * len(shape),
                                       pipeline_mode=pl.Buffered(1))
    return pl.pallas_call(
        _out_kernel,
        grid=grid,
        in_specs=[
            row_spec, row_spec, row_spec, row_spec, row_spec,
            const((D_MODEL, D_MODEL)), const((D_MODEL, D_MODEL)), const((D_MODEL, D_MODEL)),
            const((1, D_MODEL)),
            const((D_MODEL, D_FF)), const((D_FF, D_MODEL)),
            const((1, D_MODEL)), const((1, D_MODEL)),
        ],
        out_specs=row_spec,
        out_shape=jax.ShapeDtypeStruct((tokens, D_MODEL), F32),
        scratch_shapes=[pltpu.VMEM((rows, D_FF), BF16)],
        compiler_params=pltpu.CompilerParams(
            dimension_semantics=("arbitrary",), vmem_limit_bytes=VMEM_LIMIT_BYTES),
        name="out",
    )(x2, a, att, ga, gb, wa, wb, wo, gpost, wf1, wf2, gffpre, gffpost)


def _rope_inv_freq_rows():
    inv_freq = ROPE_THETA ** (-jnp.arange(0, ROPE_DIM, 2, dtype=F32) / ROPE_DIM)
    return jnp.broadcast_to(inv_freq[:, None], (ROPE_HALF, LANES))


def kernel(x, positions, w_in, ln_v_gain, ln_v_bias, w_spatial, b_spatial, sinks, w_a, w_b, w_o, norm_mix_pre, norm_mix_post, w_ff_in, w_ff_out, norm_ff_pre, norm_ff_post):
    batch, seq, _ = x.shape
    depth = w_in.shape[0]
    tokens = batch * seq
    row = lambda v: v.reshape(1, -1)
    invf = _rope_inv_freq_rows()
    pos3 = positions.reshape(tokens // PROJ_ROWS, PROJ_ROWS // CHUNK, LANES)
    x2 = x.reshape(tokens, D_MODEL)
    for l in range(depth):
        u, vn, q, kd, vd, ga, gb = _projection(
            x2, pos3, invf, row(norm_mix_pre[l]), w_in[l].astype(BF16),
            row(ln_v_gain[l]), row(ln_v_bias[l]))
        a, att = _mixing(sinks[l], u, vn, q, kd, vd, w_spatial[l], b_spatial[l].T,
                         seq // CHUNK)
        x2 = _output(x2, a, att, ga, gb, w_a[l].astype(BF16), w_b[l].astype(BF16),
                     w_o[l].astype(BF16), row(norm_mix_post[l]),
                     w_ff_in[l].astype(BF16), w_ff_out[l].astype(BF16),
                     row(norm_ff_pre[l]), row(norm_ff_post[l]))
    return x2.reshape(batch, seq, D_MODEL)
```

```python
import functools

import jax
import jax.numpy as jnp
import numpy as np
from jax import lax
from jax.experimental import pallas as pl
from jax.experimental.pallas import tpu as pltpu

D_MODEL = 1024
SGU_GROUPS = 8
CHUNK = 128
HEAD_DIM = 64
N_Q_HEADS = 16
N_KV_HEADS = 4
Q_PER_KV = N_Q_HEADS // N_KV_HEADS
KV_WIDTH = N_KV_HEADS * HEAD_DIM
ROPE_DIM = 16
ROPE_HALF = ROPE_DIM // 2
ROPE_THETA = 500000.0
D_FF = 4 * D_MODEL
EPS = 1e-6
LANES = 128
SUBLANES = 8
NEG_INF = -1e30
KV_DUP_WIDTH = N_KV_HEADS * LANES

_OFF_U = 0
_OFF_V = _OFF_U + D_MODEL
_OFF_Q = _OFF_V + D_MODEL
_OFF_K = _OFF_Q + D_MODEL
_OFF_VA = _OFF_K + KV_WIDTH
_OFF_GA = _OFF_VA + KV_WIDTH
_OFF_GB = _OFF_GA + D_MODEL
IN_WIDTH = _OFF_GB + D_MODEL

VMEM_LIMIT_BYTES = 56 * 1024 * 1024
PROJ_ROWS = 512
MIX_BLOCKS = 4
OUT_ROWS = 512
SUB_ROWS = 256
FF_CHUNK = 1024

BF16 = jnp.bfloat16
F32 = jnp.float32


def _rms(x, gain):
    return x * lax.rsqrt(jnp.mean(x * x, axis=-1, keepdims=True) + EPS) * gain


def _dot(a, b):
    return jnp.dot(a, b, preferred_element_type=F32)


def _sigmoid(x):
    return 1.0 / (1.0 + jnp.exp(-x))


def _low_half_lanes(rows):
    return lax.broadcasted_iota(jnp.int32, (rows, LANES), 1) < HEAD_DIM


def _rope_tables(pos_row, invf):
    ang = invf * pos_row
    cos = jnp.cos(ang)
    sin = jnp.sin(ang)
    pad = HEAD_DIM - ROPE_DIM
    ones = jnp.ones((pad, LANES), F32)
    zeros = jnp.zeros((pad, LANES), F32)
    cos_lt = jnp.concatenate([cos, cos, ones] * (LANES // HEAD_DIM), axis=0)
    sin_lt = jnp.concatenate([-sin, sin, zeros] * (LANES // HEAD_DIM), axis=0)
    return cos_lt.T, sin_lt.T


def _proj_kernel(x_ref, pos_ref, invf_ref, gpre_ref, w_ref, lng_ref, lnb_ref,
                 u_ref, vn_ref, q_ref, kd_ref, vd_ref, ga_ref, gb_ref):
    rows = x_ref.shape[0]
    subs = [slice(t * SUB_ROWS, (t + 1) * SUB_ROWS) for t in range(rows // SUB_ROWS)]
    hs = [_rms(x_ref[r, :], gpre_ref[...]).astype(BF16) for r in subs]

    def proj(h, lo, width):
        return _dot(h, w_ref[:, lo:lo + width])

    low = _low_half_lanes(CHUNK)
    d = lax.broadcasted_iota(jnp.int32, (CHUNK, LANES), 1) & (HEAD_DIM - 1)
    pair_next = d < ROPE_HALF
    chunks = [slice(j * CHUNK, (j + 1) * CHUNK) for j in range(rows // CHUNK)]
    tile = lambda c: slice(c * LANES, (c + 1) * LANES)
    tables = [_rope_tables(pos_ref[0, j:j + 1, :].astype(F32), invf_ref[...])
              for j in range(rows // CHUNK)]

    def rot(x, c_t, s_t):
        partner = jnp.where(pair_next, pltpu.roll(x, LANES - ROPE_HALF, axis=1),
                            pltpu.roll(x, ROPE_HALF, axis=1))
        return x * c_t + partner * s_t

    def store_dup(ref, rc, c, t):
        rolled = pltpu.roll(t, HEAD_DIM, axis=1)
        ref[rc, tile(2 * c)] = jnp.where(low, t, rolled).astype(BF16)
        ref[rc, tile(2 * c + 1)] = jnp.where(low, rolled, t).astype(BF16)

    scale = HEAD_DIM ** -0.5
    q = jnp.concatenate([proj(h, _OFF_Q, D_MODEL) for h in hs], axis=0)
    for rc, (c_tab, s_tab) in zip(chunks, tables):
        cq, sq = c_tab * scale, s_tab * scale
        for c in range(D_MODEL // LANES):
            q_ref[rc, tile(c)] = rot(q[rc, tile(c)], cq, sq).astype(BF16)
    k = jnp.concatenate([proj(h, _OFF_K, KV_WIDTH) for h in hs], axis=0)
    for rc, (c_tab, s_tab) in zip(chunks, tables):
        for c in range(KV_WIDTH // LANES):
            store_dup(kd_ref, rc, c, rot(k[rc, tile(c)], c_tab, s_tab))

    for r, h in zip(subs, hs):
        v = jax.nn.gelu(proj(h, _OFF_V, D_MODEL))
        mu = jnp.mean(v, axis=-1, keepdims=True)
        vc = v - mu
        var = jnp.mean(vc * vc, axis=-1, keepdims=True)
        vn_ref[r, :] = (vc * lax.rsqrt(var + EPS) * lng_ref[...] + lnb_ref[...]).astype(BF16)
    for r, h in zip(subs, hs):
        u_ref[r, :] = jax.nn.gelu(proj(h, _OFF_U, D_MODEL)).astype(BF16)
    for r, h in zip(subs, hs):
        ga_ref[r, :] = _sigmoid(proj(h, _OFF_GA, D_MODEL)).astype(BF16)
    for r, h in zip(subs, hs):
        gb_ref[r, :] = _sigmoid(proj(h, _OFF_GB, D_MODEL)).astype(BF16)
    va = jnp.concatenate([proj(h, _OFF_VA, KV_WIDTH) for h in hs], axis=0)
    for rc in chunks:
        for c in range(KV_WIDTH // LANES):
            store_dup(vd_ref, rc, c, va[rc, tile(c)])


def _projection(x2, pos3, invf, gpre, w_in, lng, lnb):
    tokens = x2.shape[0]
    rows = PROJ_ROWS
    grid = (tokens // rows,)
    row_spec = lambda width: pl.BlockSpec((rows, width), lambda i: (i, 0))
    full = lambda shape: pl.BlockSpec(shape, lambda i: (0,) * len(shape))
    out_widths = (D_MODEL, D_MODEL, D_MODEL, KV_DUP_WIDTH, KV_DUP_WIDTH, D_MODEL, D_MODEL)
    return pl.pallas_call(
        _proj_kernel,
        grid=grid,
        in_specs=[
            row_spec(D_MODEL),
            pl.BlockSpec((1, rows // CHUNK, LANES), lambda i: (i, 0, 0)),
            full((SUBLANES, LANES)),
            full((1, D_MODEL)),
            full((D_MODEL, IN_WIDTH)),
            full((1, D_MODEL)),
            full((1, D_MODEL)),
        ],
        out_specs=[row_spec(w) for w in out_widths],
        out_shape=[jax.ShapeDtypeStruct((tokens, w), BF16) for w in out_widths],
        compiler_params=pltpu.CompilerParams(
            dimension_semantics=("arbitrary",), vmem_limit_bytes=VMEM_LIMIT_BYTES),
        name="proj",
    )(x2, pos3, invf, gpre, w_in, lng, lnb)


def _mix_kernel(blocks_per_seq, sinks_ref, u_ref, vn_ref, q_ref, kd_ref, vd_ref,
                kdp_ref, vdp_ref, ws_ref, bs_ref, a_ref, att_ref, wtril_ref):
    i = pl.program_id(0)
    row = lax.broadcasted_iota(jnp.int32, (CHUNK, CHUNK), 0)
    col = lax.broadcasted_iota(jnp.int32, (CHUNK, CHUNK), 1)
    causal = col <= row

    @pl.when(i == 0)
    def _():
        for g in range(SGU_GROUPS):
            wtril_ref[g] = jnp.where(causal, ws_ref[g], 0.0).astype(BF16)

    blk = lambda b: slice(b * CHUNK, (b + 1) * CHUNK)
    tile = lambda c: slice(c * LANES, (c + 1) * LANES)

    for g in range(SGU_GROUPS):
        bias = bs_ref[:, g:g + 1]
        for bp in range(MIX_BLOCKS // 2):
            b0, b1 = 2 * bp, 2 * bp + 1
            rhs = jnp.concatenate([vn_ref[blk(b0), tile(g)], vn_ref[blk(b1), tile(g)]], axis=1)
            mixed = _dot(wtril_ref[g], rhs) + bias
            a_ref[blk(b0), tile(g)] = (u_ref[blk(b0), tile(g)].astype(F32)
                                       * mixed[:, :LANES]).astype(BF16)
            a_ref[blk(b1), tile(g)] = (u_ref[blk(b1), tile(g)].astype(F32)
                                       * mixed[:, LANES:]).astype(BF16)

    first = ((i * MIX_BLOCKS) % blocks_per_seq) == 0
    prev_bias = jnp.where(first, NEG_INF, 0.0).astype(F32)
    low = col < HEAD_DIM
    ones = jnp.ones((2 * CHUNK, LANES), BF16)

    for b in range(MIX_BLOCKS):
        for g in range(N_KV_HEADS):
            if b == 0:
                k_prev, v_prev = kdp_ref[:, tile(g)], vdp_ref[:, tile(g)]
            else:
                k_prev, v_prev = kd_ref[blk(b - 1), tile(g)], vd_ref[blk(b - 1), tile(g)]
            kband = jnp.concatenate([k_prev, kd_ref[blk(b), tile(g)]], axis=0)
            vband = jnp.concatenate(
                [jnp.concatenate([v_prev, vd_ref[blk(b), tile(g)]], axis=0), ones], axis=1)
            q_tiles = []
            for t in range(Q_PER_KV // 2):
                qt = q_ref[blk(b), tile(2 * g + t)]
                zero = jnp.zeros_like(qt)
                q_tiles += [jnp.where(low, qt, zero), jnp.where(low, zero, qt)]
            q4 = jnp.concatenate(q_tiles, axis=0)
            s4 = lax.dot_general(q4, kband, (((1,), (1,)), ((), ())),
                                 preferred_element_type=F32)
            p_rows, sink_terms = [], []
            for hh in range(Q_PER_KV):
                s_prev = s4[blk(hh), :CHUNK]
                if b == 0:
                    s_prev = s_prev + prev_bias
                s = jnp.where(causal, s4[blk(hh), CHUNK:], s_prev)
                sink = sinks_ref[Q_PER_KV * g + hh]
                m = jnp.maximum(jnp.max(s, axis=-1, keepdims=True), sink)
                p = jnp.exp(s - m)
                pz = jnp.zeros_like(p)
                p_rows.append(jnp.concatenate(
                    [jnp.where(causal, pz, p), jnp.where(causal, p, pz)], axis=1).astype(BF16))
                sink_terms.append(jnp.exp(sink - m))
            pband = jnp.concatenate(p_rows, axis=0)
            o4 = _dot(pband, vband)
            outs = []
            for hh in range(Q_PER_KV):
                denom = o4[blk(hh), LANES:] + sink_terms[hh]
                outs.append(o4[blk(hh), :LANES] * (1.0 / denom))
            for t in range(Q_PER_KV // 2):
                att_ref[blk(b), tile(2 * g + t)] = jnp.where(
                    low, outs[2 * t], outs[2 * t + 1]).astype(BF16)


def _mixing(sinks, u, vn, q, kd, vd, w_spatial, b_spatial_t, blocks_per_seq):
    tokens = u.shape[0]
    rows = MIX_BLOCKS * CHUNK
    grid = (tokens // rows,)
    blk = lambda width: pl.BlockSpec((rows, width), lambda i: (i, 0))
    prev = pl.BlockSpec((CHUNK, KV_DUP_WIDTH),
                        lambda i: (jnp.maximum(i * MIX_BLOCKS - 1, 0), 0))
    return pl.pallas_call(
        functools.partial(_mix_kernel, blocks_per_seq),
        grid=grid,
        in_specs=[
            pl.BlockSpec(memory_space=pltpu.SMEM),
            blk(D_MODEL), blk(D_MODEL), blk(D_MODEL), blk(KV_DUP_WIDTH), blk(KV_DUP_WIDTH),
            prev, prev,
            pl.BlockSpec((SGU_GROUPS, CHUNK, CHUNK), lambda i: (0, 0, 0)),
            pl.BlockSpec((CHUNK, SGU_GROUPS), lambda i: (0, 0)),
        ],
        out_specs=[blk(D_MODEL), blk(D_MODEL)],
        out_shape=[jax.ShapeDtypeStruct((tokens, D_MODEL), BF16)] * 2,
        scratch_shapes=[pltpu.VMEM((SGU_GROUPS, CHUNK, CHUNK), BF16)],
        compiler_params=pltpu.CompilerParams(
            dimension_semantics=("arbitrary",), vmem_limit_bytes=VMEM_LIMIT_BYTES),
        name="mix",
    )(sinks, u, vn, q, kd, vd, kd, vd, w_spatial, b_spatial_t)


def _out_kernel(x_ref, a_ref, att_ref, ga_ref, gb_ref, wa_ref, wb_ref, wo_ref,
                gpost_ref, wf1_ref, wf2_ref, gffpre_ref, gffpost_ref, o_ref, h1_ref):
    subs = [slice(t * SUB_ROWS, (t + 1) * SUB_ROWS) for t in range(x_ref.shape[0] // SUB_ROWS)]
    merged = [(ga_ref[r, :].astype(F32) * _dot(a_ref[r, :], wa_ref[...])
               + gb_ref[r, :].astype(F32) * _dot(att_ref[r, :], wb_ref[...])).astype(BF16)
              for r in subs]
    mix = [_dot(m, wo_ref[...]) for m in merged]
    x1 = [x_ref[r, :] + _rms(mx, gpost_ref[...]) for r, mx in zip(subs, mix)]
    hf = [_rms(x, gffpre_ref[...]).astype(BF16) for x in x1]
    for c in range(D_FF // FF_CHUNK):
        sl = slice(c * FF_CHUNK, (c + 1) * FF_CHUNK)
        for r, h in zip(subs, hf):
            h1 = jnp.maximum(_dot(h, wf1_ref[:, sl]), 0.0)
            h1_ref[r, sl] = (h1 * h1).astype(BF16)
    ff = [_dot(h1_ref[r, :], wf2_ref[...]) for r in subs]
    for r, x, f in zip(subs, x1, ff):
        o_ref[r, :] = x + _rms(f, gffpost_ref[...])


def _output(x2, a, att, ga, gb, wa, wb, wo, gpost, wf1, wf2, gffpre, gffpost):
    tokens = x2.shape[0]
    rows = OUT_ROWS
    grid = (tokens // rows,)
    row_spec = pl.BlockSpec((rows, D_MODEL), lambda i: (i, 0))
    const = lambda shape: pl.BlockSpec(shape, lambda i: (0,) * len(shape),
                                       pipeline_mode=pl.Buffered(1))
    return pl.pallas_call(
        _out_kernel,
        grid=grid,
        in_specs=[
            row_spec, row_spec, row_spec, row_spec, row_spec,
            const((D_MODEL, D_MODEL)), const((D_MODEL, D_MODEL)), const((D_MODEL, D_MODEL)),
            const((1, D_MODEL)),
            const((D_MODEL, D_FF)), const((D_FF, D_MODEL)),
            const((1, D_MODEL)), const((1, D_MODEL)),
        ],
        out_specs=row_spec,
        out_shape=jax.ShapeDtypeStruct((tokens, D_MODEL), F32),
        scratch_shapes=[pltpu.VMEM((rows, D_FF), BF16)],
        compiler_params=pltpu.CompilerParams(
            dimension_semantics=("arbitrary",), vmem_limit_bytes=VMEM_LIMIT_BYTES),
        name="out",
    )(x2, a, att, ga, gb, wa, wb, wo, gpost, wf1, wf2, gffpre, gffpost)


def _rope_inv_freq_rows():
    inv_freq = ROPE_THETA ** (-jnp.arange(0, ROPE_DIM, 2, dtype=F32) / ROPE_DIM)
    return jnp.broadcast_to(inv_freq[:, None], (ROPE_HALF, LANES))


def kernel(x, positions, w_in, ln_v_gain, ln_v_bias, w_spatial, b_spatial, sinks, w_a, w_b, w_o, norm_mix_pre, norm_mix_post, w_ff_in, w_ff_out, norm_ff_pre, norm_ff_post):
    batch, seq, _ = x.shape
    depth = w_in.shape[0]
    tokens = batch * seq
    row = lambda v: v.reshape(1, -1)
    invf = _rope_inv_freq_rows()
    pos3 = positions.reshape(tokens // PROJ_ROWS, PROJ_ROWS // CHUNK, LANES)
    x2 = x.reshape(tokens, D_MODEL)
    for l in range(depth):
        u, vn, q, kd, vd, ga, gb = _projection(
            x2, pos3, invf, row(norm_mix_pre[l]), w_in[l].astype(BF16),
            row(ln_v_gain[l]), row(ln_v_bias[l]))
        a, att = _mixing(sinks[l], u, vn, q, kd, vd, w_spatial[l], b_spatial[l].T,
                         seq // CHUNK)
        x2 = _output(x2, a, att, ga, gb, w_a[l].astype(BF16), w_b[l].astype(BF16),
                     w_o[l].astype(BF16), row(norm_mix_post[l]),
                     w_ff_in[l].astype(BF16), w_ff_out[l].astype(BF16),
                     row(norm_ff_pre[l]), row(norm_ff_post[l]))
    return x2.reshape(batch, seq, D_MODEL)
```

```python
import functools
import math

import jax
import jax.numpy as jnp
import numpy as np
from jax import lax
from jax.experimental import pallas as pl
from jax.experimental.pallas import tpu as pltpu

D_MODEL = 1024
SGU_GROUPS = 8
CHUNK = 128
HEAD_DIM = 64
N_Q_HEADS = 16
N_KV_HEADS = 4
Q_PER_KV = N_Q_HEADS // N_KV_HEADS
KV_WIDTH = N_KV_HEADS * HEAD_DIM
ROPE_DIM = 16
ROPE_HALF = ROPE_DIM // 2
ROPE_THETA = 500000.0
D_FF = 4 * D_MODEL
EPS = 1e-6
LANES = 128
SUBLANES = 8
NEG_INF = -1e30
KV_DUP_WIDTH = N_KV_HEADS * LANES

_OFF_U = 0
_OFF_V = _OFF_U + D_MODEL
_OFF_Q = _OFF_V + D_MODEL
_OFF_K = _OFF_Q + D_MODEL
_OFF_VA = _OFF_K + KV_WIDTH
_OFF_GA = _OFF_VA + KV_WIDTH
_OFF_GB = _OFF_GA + D_MODEL
IN_WIDTH = _OFF_GB + D_MODEL

VMEM_LIMIT_BYTES = 56 * 1024 * 1024
PROJ_ROWS = 512
MIX_BLOCKS = 4
OUT_ROWS = 512
SUB_ROWS = 256
FF_CHUNK = 1024

BF16 = jnp.bfloat16
F32 = jnp.float32


def _rms(x, gain):
    return x * lax.rsqrt(jnp.mean(x * x, axis=-1, keepdims=True) + EPS) * gain


def _dot(a, b):
    return jnp.dot(a, b, preferred_element_type=F32)


def _sigmoid(x):
    return 1.0 / (1.0 + jnp.exp(-x))


_GELU_K1 = -2.0 * math.sqrt(2.0 / math.pi) * math.log2(math.e)
_GELU_K3 = _GELU_K1 * 0.044715


def _gelu_tanh(x):
    return x / (1.0 + jnp.exp2(x * (x * x * _GELU_K3 + _GELU_K1)))


def _low_half_lanes(rows):
    return lax.broadcasted_iota(jnp.int32, (rows, LANES), 1) < HEAD_DIM


def _rope_tables(pos_row, invf):
    ang = invf * pos_row
    cos = jnp.cos(ang)
    sin = jnp.sin(ang)
    pad = HEAD_DIM - ROPE_DIM
    ones = jnp.ones((pad, LANES), F32)
    zeros = jnp.zeros((pad, LANES), F32)
    cos_lt = jnp.concatenate([cos, cos, ones] * (LANES // HEAD_DIM), axis=0)
    sin_lt = jnp.concatenate([-sin, sin, zeros] * (LANES // HEAD_DIM), axis=0)
    return cos_lt.T, sin_lt.T


def _proj_kernel(x_ref, pos_ref, invf_ref, gpre_ref, w_ref, lng_ref, lnb_ref, *refs):
    n_side = len(refs) - 7 >> 1
    side_in, outs, side_out = refs[:n_side], refs[n_side:n_side + 7], refs[n_side + 7:]
    u_ref, vn_ref, q_ref, kd_ref, vd_ref, ga_ref, gb_ref = outs
    for src, dst in zip(side_in, side_out):
        dst[...] = src[...].astype(BF16)
    rows = x_ref.shape[0]
    subs = [slice(t * SUB_ROWS, (t + 1) * SUB_ROWS) for t in range(rows // SUB_ROWS)]
    hs = [_rms(x_ref[r, :], gpre_ref[...]).astype(BF16) for r in subs]

    def proj(h, lo, width):
        return _dot(h, w_ref[:, lo:lo + width])

    low = _low_half_lanes(CHUNK)
    d = lax.broadcasted_iota(jnp.int32, (CHUNK, LANES), 1) & (HEAD_DIM - 1)
    pair_next = d < ROPE_HALF
    chunks = [slice(j * CHUNK, (j + 1) * CHUNK) for j in range(rows // CHUNK)]
    tile = lambda c: slice(c * LANES, (c + 1) * LANES)
    tables = [_rope_tables(pos_ref[0, j:j + 1, :].astype(F32), invf_ref[...])
              for j in range(rows // CHUNK)]

    def rot(x, c_t, s_t):
        partner = jnp.where(pair_next, pltpu.roll(x, LANES - ROPE_HALF, axis=1),
                            pltpu.roll(x, ROPE_HALF, axis=1))
        return x * c_t + partner * s_t

    def store_dup(ref, rc, c, t):
        rolled = pltpu.roll(t, HEAD_DIM, axis=1)
        ref[rc, tile(2 * c)] = jnp.where(low, t, rolled).astype(BF16)
        ref[rc, tile(2 * c + 1)] = jnp.where(low, rolled, t).astype(BF16)

    scale = HEAD_DIM ** -0.5
    q = jnp.concatenate([proj(h, _OFF_Q, D_MODEL) for h in hs], axis=0)
    for rc, (c_tab, s_tab) in zip(chunks, tables):
        cq, sq = c_tab * scale, s_tab * scale
        for c in range(D_MODEL // LANES):
            q_ref[rc, tile(c)] = rot(q[rc, tile(c)], cq, sq).astype(BF16)
    k = jnp.concatenate([proj(h, _OFF_K, KV_WIDTH) for h in hs], axis=0)
    for rc, (c_tab, s_tab) in zip(chunks, tables):
        for c in range(KV_WIDTH // LANES):
            store_dup(kd_ref, rc, c, rot(k[rc, tile(c)], c_tab, s_tab))

    for r, h in zip(subs, hs):
        v = _gelu_tanh(proj(h, _OFF_V, D_MODEL))
        mu = jnp.mean(v, axis=-1, keepdims=True)
        vc = v - mu
        var = jnp.mean(vc * vc, axis=-1, keepdims=True)
        vn_ref[r, :] = (vc * lax.rsqrt(var + EPS) * lng_ref[...] + lnb_ref[...]).astype(BF16)
    for r, h in zip(subs, hs):
        u_ref[r, :] = _gelu_tanh(proj(h, _OFF_U, D_MODEL)).astype(BF16)
    for r, h in zip(subs, hs):
        ga_ref[r, :] = _sigmoid(proj(h, _OFF_GA, D_MODEL)).astype(BF16)
    for r, h in zip(subs, hs):
        gb_ref[r, :] = _sigmoid(proj(h, _OFF_GB, D_MODEL)).astype(BF16)
    va = jnp.concatenate([proj(h, _OFF_VA, KV_WIDTH) for h in hs], axis=0)
    for rc in chunks:
        for c in range(KV_WIDTH // LANES):
            store_dup(vd_ref, rc, c, va[rc, tile(c)])


def _projection(x2, pos3, invf, gpre, w_in, lng, lnb, side_weights):
    tokens = x2.shape[0]
    rows = PROJ_ROWS
    steps = tokens // rows
    row_spec = lambda width: pl.BlockSpec((rows, width), lambda i: (i, 0))
    full = lambda shape: pl.BlockSpec(shape, lambda i: (0,) * len(shape))
    slab = lambda w: pl.BlockSpec((w.shape[0] // steps, w.shape[1]), lambda i: (i, 0))
    out_widths = (D_MODEL, D_MODEL, D_MODEL, KV_DUP_WIDTH, KV_DUP_WIDTH, D_MODEL, D_MODEL)
    return pl.pallas_call(
        _proj_kernel,
        grid=(steps,),
        in_specs=[
            row_spec(D_MODEL),
            pl.BlockSpec((1, rows // CHUNK, LANES), lambda i: (i, 0, 0)),
            full((SUBLANES, LANES)),
            full((1, D_MODEL)),
            full((D_MODEL, IN_WIDTH)),
            full((1, D_MODEL)),
            full((1, D_MODEL)),
        ] + [slab(w) for w in side_weights],
        out_specs=[row_spec(w) for w in out_widths] + [slab(w) for w in side_weights],
        out_shape=([jax.ShapeDtypeStruct((tokens, w), BF16) for w in out_widths]
                   + [jax.ShapeDtypeStruct(w.shape, BF16) for w in side_weights]),
        compiler_params=pltpu.CompilerParams(
            dimension_semantics=("arbitrary",), vmem_limit_bytes=VMEM_LIMIT_BYTES),
        name="proj",
    )(x2, pos3, invf, gpre, w_in, lng, lnb, *side_weights)


def _mix_kernel(blocks_per_seq, sinks_ref, u_ref, vn_ref, q_ref, kd_ref, vd_ref,
                kdp_ref, vdp_ref, ws_ref, bs_ref, a_ref, att_ref, wtril_ref):
    i = pl.program_id(0)
    row = lax.broadcasted_iota(jnp.int32, (CHUNK, CHUNK), 0)
    col = lax.broadcasted_iota(jnp.int32, (CHUNK, CHUNK), 1)
    causal = col <= row

    @pl.when(i == 0)
    def _():
        for g in range(SGU_GROUPS):
            wtril_ref[g] = jnp.where(causal, ws_ref[g], 0.0).astype(BF16)

    blk = lambda b: slice(b * CHUNK, (b + 1) * CHUNK)
    tile = lambda c: slice(c * LANES, (c + 1) * LANES)

    for g in range(SGU_GROUPS):
        bias = bs_ref[:, g:g + 1]
        for bp in range(MIX_BLOCKS // 2):
            b0, b1 = 2 * bp, 2 * bp + 1
            rhs = jnp.concatenate([vn_ref[blk(b0), tile(g)], vn_ref[blk(b1), tile(g)]], axis=1)
            mixed = _dot(wtril_ref[g], rhs) + bias
            a_ref[blk(b0), tile(g)] = (u_ref[blk(b0), tile(g)].astype(F32)
                                       * mixed[:, :LANES]).astype(BF16)
            a_ref[blk(b1), tile(g)] = (u_ref[blk(b1), tile(g)].astype(F32)
                                       * mixed[:, LANES:]).astype(BF16)

    first = ((i * MIX_BLOCKS) % blocks_per_seq) == 0
    prev_bias = jnp.where(first, NEG_INF, 0.0).astype(F32)
    low = col < HEAD_DIM
    ones = jnp.ones((2 * CHUNK, LANES), BF16)

    for b in range(MIX_BLOCKS):
        for g in range(N_KV_HEADS):
            if b == 0:
                k_prev, v_prev = kdp_ref[:, tile(g)], vdp_ref[:, tile(g)]
            else:
                k_prev, v_prev = kd_ref[blk(b - 1), tile(g)], vd_ref[blk(b - 1), tile(g)]
            kband = jnp.concatenate([k_prev, kd_ref[blk(b), tile(g)]], axis=0)
            vband = jnp.concatenate(
                [jnp.concatenate([v_prev, vd_ref[blk(b), tile(g)]], axis=0), ones], axis=1)
            q_tiles = []
            for t in range(Q_PER_KV // 2):
                qt = q_ref[blk(b), tile(2 * g + t)]
                zero = jnp.zeros_like(qt)
                q_tiles += [jnp.where(low, qt, zero), jnp.where(low, zero, qt)]
            q4 = jnp.concatenate(q_tiles, axis=0)
            s4 = lax.dot_general(q4, kband, (((1,), (1,)), ((), ())),
                                 preferred_element_type=F32)
            p_rows, sink_terms = [], []
            for hh in range(Q_PER_KV):
                s_prev = s4[blk(hh), :CHUNK]
                if b == 0:
                    s_prev = s_prev + prev_bias
                s = jnp.where(causal, s4[blk(hh), CHUNK:], s_prev)
                sink = sinks_ref[Q_PER_KV * g + hh]
                m = jnp.maximum(jnp.max(s, axis=-1, keepdims=True), sink)
                p = jnp.exp(s - m)
                pz = jnp.zeros_like(p)
                p_rows.append(jnp.concatenate(
                    [jnp.where(causal, pz, p), jnp.where(causal, p, pz)], axis=1).astype(BF16))
                sink_terms.append(jnp.exp(sink - m))
            pband = jnp.concatenate(p_rows, axis=0)
            o4 = _dot(pband, vband)
            outs = []
            for hh in range(Q_PER_KV):
                denom = o4[blk(hh), LANES:] + sink_terms[hh]
                outs.append(o4[blk(hh), :LANES] * (1.0 / denom))
            for t in range(Q_PER_KV // 2):
                att_ref[blk(b), tile(2 * g + t)] = jnp.where(
                    low, outs[2 * t], outs[2 * t + 1]).astype(BF16)


def _mixing(sinks, u, vn, q, kd, vd, w_spatial, b_spatial_t, blocks_per_seq):
    tokens = u.shape[0]
    rows = MIX_BLOCKS * CHUNK
    grid = (tokens // rows,)
    blk = lambda width: pl.BlockSpec((rows, width), lambda i: (i, 0))
    prev = pl.BlockSpec((CHUNK, KV_DUP_WIDTH),
                        lambda i: (jnp.maximum(i * MIX_BLOCKS - 1, 0), 0))
    return pl.pallas_call(
        functools.partial(_mix_kernel, blocks_per_seq),
        grid=grid,
        in_specs=[
            pl.BlockSpec(memory_space=pltpu.SMEM),
            blk(D_MODEL), blk(D_MODEL), blk(D_MODEL), blk(KV_DUP_WIDTH), blk(KV_DUP_WIDTH),
            prev, prev,
            pl.BlockSpec((SGU_GROUPS, CHUNK, CHUNK), lambda i: (0, 0, 0)),
            pl.BlockSpec((CHUNK, SGU_GROUPS), lambda i: (0, 0)),
        ],
        out_specs=[blk(D_MODEL), blk(D_MODEL)],
        out_shape=[jax.ShapeDtypeStruct((tokens, D_MODEL), BF16)] * 2,
        scratch_shapes=[pltpu.VMEM((SGU_GROUPS, CHUNK, CHUNK), BF16)],
        compiler_params=pltpu.CompilerParams(
            dimension_semantics=("arbitrary",), vmem_limit_bytes=VMEM_LIMIT_BYTES),
        name="mix",
    )(sinks, u, vn, q, kd, vd, kd, vd, w_spatial, b_spatial_t)


def _out_kernel(x_ref, a_ref, att_ref, ga_ref, gb_ref, wa_ref, wb_ref, wo_ref,
                gpost_ref, wf1_ref, wf2_ref, gffpre_ref, gffpost_ref, o_ref, h1_ref):
    subs = [slice(t * SUB_ROWS, (t + 1) * SUB_ROWS) for t in range(x_ref.shape[0] // SUB_ROWS)]
    merged = [(ga_ref[r, :].astype(F32) * _dot(a_ref[r, :], wa_ref[...])
               + gb_ref[r, :].astype(F32) * _dot(att_ref[r, :], wb_ref[...])).astype(BF16)
              for r in subs]
    mix = [_dot(m, wo_ref[...]) for m in merged]
    x1 = [x_ref[r, :] + _rms(mx, gpost_ref[...]) for r, mx in zip(subs, mix)]
    hf = [_rms(x, gffpre_ref[...]).astype(BF16) for x in x1]
    for c in range(D_FF // FF_CHUNK):
        sl = slice(c * FF_CHUNK, (c + 1) * FF_CHUNK)
        for r, h in zip(subs, hf):
            h1 = jnp.maximum(_dot(h, wf1_ref[:, sl]), 0.0)
            h1_ref[r, sl] = (h1 * h1).astype(BF16)
    ff = [_dot(h1_ref[r, :], wf2_ref[...]) for r in subs]
    for r, x, f in zip(subs, x1, ff):
        o_ref[r, :] = x + _rms(f, gffpost_ref[...])


def _output(x2, a, att, ga, gb, wa, wb, wo, gpost, wf1, wf2, gffpre, gffpost):
    tokens = x2.shape[0]
    rows = OUT_ROWS
    grid = (tokens // rows,)
    row_spec = pl.BlockSpec((rows, D_MODEL), lambda i: (i, 0))
    const = lambda shape: pl.BlockSpec(shape, lambda i: (0,) * len(shape),
                                       pipeline_mode=pl.Buffered(1))
    return pl.pallas_call(
        _out_kernel,
        grid=grid,
        in_specs=[
            row_spec, row_spec, row_spec, row_spec, row_spec,
            const((D_MODEL, D_MODEL)), const((D_MODEL, D_MODEL)), const((D_MODEL, D_MODEL)),
            const((1, D_MODEL)),
            const((D_MODEL, D_FF)), const((D_FF, D_MODEL)),
            const((1, D_MODEL)), const((1, D_MODEL)),
        ],
        out_specs=row_spec,
        out_shape=jax.ShapeDtypeStruct((tokens, D_MODEL), F32),
        scratch_shapes=[pltpu.VMEM((rows, D_FF), BF16)],
        compiler_params=pltpu.CompilerParams(
            dimension_semantics=("arbitrary",), vmem_limit_bytes=VMEM_LIMIT_BYTES),
        name="out",
    )(x2, a, att, ga, gb, wa, wb, wo, gpost, wf1, wf2, gffpre, gffpost)


def _rope_inv_freq_rows():
    inv_freq = ROPE_THETA ** (-jnp.arange(0, ROPE_DIM, 2, dtype=F32) / ROPE_DIM)
    return jnp.broadcast_to(inv_freq[:, None], (ROPE_HALF, LANES))


def kernel(x, positions, w_in, ln_v_gain, ln_v_bias, w_spatial, b_spatial, sinks, w_a, w_b, w_o, norm_mix_pre, norm_mix_post, w_ff_in, w_ff_out, norm_ff_pre, norm_ff_post):
    batch, seq, _ = x.shape
    depth = w_in.shape[0]
    tokens = batch * seq
    row = lambda v: v.reshape(1, -1)
    invf = _rope_inv_freq_rows()
    pos3 = positions.reshape(tokens // PROJ_ROWS, PROJ_ROWS // CHUNK, LANES)
    x2 = x.reshape(tokens, D_MODEL)
    for l in range(depth):
        u, vn, q, kd, vd, ga, gb, wa, wb, wo, wf1, wf2 = _projection(
            x2, pos3, invf, row(norm_mix_pre[l]), w_in[l].astype(BF16),
            row(ln_v_gain[l]), row(ln_v_bias[l]),
            (w_a[l], w_b[l], w_o[l], w_ff_in[l], w_ff_out[l]))
        a, att = _mixing(sinks[l], u, vn, q, kd, vd, w_spatial[l], b_spatial[l].T,
                         seq // CHUNK)
        x2 = _output(x2, a, att, ga, gb, wa, wb, wo, row(norm_mix_post[l]), wf1, wf2,
                     row(norm_ff_pre[l]), row(norm_ff_post[l]))
    return x2.reshape(batch, seq, D_MODEL)
```

```python
import functools
import math

import jax
import jax.numpy as jnp
import numpy as np
from jax import lax
from jax.experimental import pallas as pl
from jax.experimental.pallas import tpu as pltpu

D_MODEL = 1024
SGU_GROUPS = 8
CHUNK = 128
HEAD_DIM = 64
N_Q_HEADS = 16
N_KV_HEADS = 4
Q_PER_KV = N_Q_HEADS // N_KV_HEADS
KV_WIDTH = N_KV_HEADS * HEAD_DIM
ROPE_DIM = 16
ROPE_HALF = ROPE_DIM // 2
ROPE_THETA = 500000.0
D_FF = 4 * D_MODEL
EPS = 1e-6
LANES = 128
SUBLANES = 8
NEG_INF = -1e30
KV_DUP_WIDTH = N_KV_HEADS * LANES

_OFF_U = 0
_OFF_V = _OFF_U + D_MODEL
_OFF_Q = _OFF_V + D_MODEL
_OFF_K = _OFF_Q + D_MODEL
_OFF_VA = _OFF_K + KV_WIDTH
_OFF_GA = _OFF_VA + KV_WIDTH
_OFF_GB = _OFF_GA + D_MODEL
IN_WIDTH = _OFF_GB + D_MODEL

VMEM_LIMIT_BYTES = 56 * 1024 * 1024
PROJ_ROWS = 512
MIX_BLOCKS = 8
OUT_ROWS = 512
SUB_ROWS = 256
FF_CHUNK = 1024

BF16 = jnp.bfloat16
F32 = jnp.float32


def _rms(x, gain):
    return x * lax.rsqrt(jnp.mean(x * x, axis=-1, keepdims=True) + EPS) * gain


def _dot(a, b):
    return jnp.dot(a, b, preferred_element_type=F32)


LOG2E = math.log2(math.e)


def _sigmoid(x):
    return 1.0 / (1.0 + jnp.exp2(x * -LOG2E))


_GELU_K1 = -2.0 * math.sqrt(2.0 / math.pi) * math.log2(math.e)
_GELU_K3 = _GELU_K1 * 0.044715


def _gelu_tanh(x):
    return x / (1.0 + jnp.exp2(x * (x * x * _GELU_K3 + _GELU_K1)))


def _low_half_lanes(rows):
    return lax.broadcasted_iota(jnp.int32, (rows, LANES), 1) < HEAD_DIM


def _rope_tables(pos_row, invf):
    ang = invf * pos_row
    cos = jnp.cos(ang)
    sin = jnp.sin(ang)
    pad = HEAD_DIM - ROPE_DIM
    ones = jnp.ones((pad, LANES), F32)
    zeros = jnp.zeros((pad, LANES), F32)
    cos_lt = jnp.concatenate([cos, cos, ones] * (LANES // HEAD_DIM), axis=0)
    sin_lt = jnp.concatenate([-sin, sin, zeros] * (LANES // HEAD_DIM), axis=0)
    return cos_lt.T, sin_lt.T


def _proj_kernel(x_ref, pos_ref, invf_ref, gpre_ref, w_ref, lng_ref, lnb_ref, *refs):
    n_side = len(refs) - 7 >> 1
    side_in, outs, side_out = refs[:n_side], refs[n_side:n_side + 7], refs[n_side + 7:]
    u_ref, vn_ref, q_ref, kd_ref, vd_ref, ga_ref, gb_ref = outs
    for src, dst in zip(side_in, side_out):
        dst[...] = src[...].astype(BF16)
    rows = x_ref.shape[0]
    subs = [slice(t * SUB_ROWS, (t + 1) * SUB_ROWS) for t in range(rows // SUB_ROWS)]
    hs = [_rms(x_ref[r, :], gpre_ref[...]).astype(BF16) for r in subs]

    def proj(h, lo, width):
        return _dot(h, w_ref[:, lo:lo + width])

    low = _low_half_lanes(CHUNK)
    d = lax.broadcasted_iota(jnp.int32, (CHUNK, LANES), 1) & (HEAD_DIM - 1)
    pair_next = d < ROPE_HALF
    chunks = [slice(j * CHUNK, (j + 1) * CHUNK) for j in range(rows // CHUNK)]
    tile = lambda c: slice(c * LANES, (c + 1) * LANES)
    tables = [_rope_tables(pos_ref[0, j:j + 1, :].astype(F32), invf_ref[...])
              for j in range(rows // CHUNK)]

    def rot(x, c_t, s_t):
        partner = jnp.where(pair_next, pltpu.roll(x, LANES - ROPE_HALF, axis=1),
                            pltpu.roll(x, ROPE_HALF, axis=1))
        return x * c_t + partner * s_t

    def store_dup(ref, rc, c, t):
        rolled = pltpu.roll(t, HEAD_DIM, axis=1)
        ref[rc, tile(2 * c)] = jnp.where(low, t, rolled).astype(BF16)
        ref[rc, tile(2 * c + 1)] = jnp.where(low, rolled, t).astype(BF16)

    scale = HEAD_DIM ** -0.5 * LOG2E
    q = jnp.concatenate([proj(h, _OFF_Q, D_MODEL) for h in hs], axis=0)
    for rc, (c_tab, s_tab) in zip(chunks, tables):
        cq, sq = c_tab * scale, s_tab * scale
        for c in range(D_MODEL // LANES):
            q_ref[rc, tile(c)] = rot(q[rc, tile(c)], cq, sq).astype(BF16)
    k = jnp.concatenate([proj(h, _OFF_K, KV_WIDTH) for h in hs], axis=0)
    for rc, (c_tab, s_tab) in zip(chunks, tables):
        for c in range(KV_WIDTH // LANES):
            store_dup(kd_ref, rc, c, rot(k[rc, tile(c)], c_tab, s_tab))

    for r, h in zip(subs, hs):
        v = _gelu_tanh(proj(h, _OFF_V, D_MODEL))
        mu = jnp.mean(v, axis=-1, keepdims=True)
        vc = v - mu
        var = jnp.mean(vc * vc, axis=-1, keepdims=True)
        vn_ref[r, :] = (vc * lax.rsqrt(var + EPS) * lng_ref[...] + lnb_ref[...]).astype(BF16)
    for r, h in zip(subs, hs):
        u_ref[r, :] = _gelu_tanh(proj(h, _OFF_U, D_MODEL)).astype(BF16)
    for r, h in zip(subs, hs):
        ga_ref[r, :] = _sigmoid(proj(h, _OFF_GA, D_MODEL)).astype(BF16)
    for r, h in zip(subs, hs):
        gb_ref[r, :] = _sigmoid(proj(h, _OFF_GB, D_MODEL)).astype(BF16)
    va = jnp.concatenate([proj(h, _OFF_VA, KV_WIDTH) for h in hs], axis=0)
    for rc in chunks:
        for c in range(KV_WIDTH // LANES):
            store_dup(vd_ref, rc, c, va[rc, tile(c)])


def _projection(x2, pos3, invf, gpre, w_in, lng, lnb, side_weights):
    tokens = x2.shape[0]
    rows = PROJ_ROWS
    steps = tokens // rows
    row_spec = lambda width: pl.BlockSpec((rows, width), lambda i: (i, 0))
    full = lambda shape: pl.BlockSpec(shape, lambda i: (0,) * len(shape))
    slab = lambda w: pl.BlockSpec((w.shape[0] // steps, w.shape[1]), lambda i: (i, 0))
    out_widths = (D_MODEL, D_MODEL, D_MODEL, KV_DUP_WIDTH, KV_DUP_WIDTH, D_MODEL, D_MODEL)
    return pl.pallas_call(
        _proj_kernel,
        grid=(steps,),
        in_specs=[
            row_spec(D_MODEL),
            pl.BlockSpec((1, rows // CHUNK, LANES), lambda i: (i, 0, 0)),
            full((SUBLANES, LANES)),
            full((1, D_MODEL)),
            full((D_MODEL, IN_WIDTH)),
            full((1, D_MODEL)),
            full((1, D_MODEL)),
        ] + [slab(w) for w in side_weights],
        out_specs=[row_spec(w) for w in out_widths] + [slab(w) for w in side_weights],
        out_shape=([jax.ShapeDtypeStruct((tokens, w), BF16) for w in out_widths]
                   + [jax.ShapeDtypeStruct(w.shape, BF16) for w in side_weights]),
        compiler_params=pltpu.CompilerParams(
            dimension_semantics=("arbitrary",), vmem_limit_bytes=VMEM_LIMIT_BYTES),
        name="proj",
    )(x2, pos3, invf, gpre, w_in, lng, lnb, *side_weights)


def _mix_kernel(blocks_per_seq, sinks_ref, u_ref, vn_ref, q_ref, kd_ref, vd_ref,
                kdp_ref, vdp_ref, ws_ref, bs_ref, a_ref, att_ref, wtril_ref):
    i = pl.program_id(0)
    row = lax.broadcasted_iota(jnp.int32, (CHUNK, CHUNK), 0)
    col = lax.broadcasted_iota(jnp.int32, (CHUNK, CHUNK), 1)
    causal = col <= row

    @pl.when(i == 0)
    def _():
        for g in range(SGU_GROUPS):
            wtril_ref[g] = jnp.where(causal, ws_ref[g], 0.0).astype(BF16)

    blk = lambda b: slice(b * CHUNK, (b + 1) * CHUNK)
    tile = lambda c: slice(c * LANES, (c + 1) * LANES)

    def sgu_piece(g, bp):
        b0, b1 = 2 * bp, 2 * bp + 1
        rhs = jnp.concatenate([vn_ref[blk(b0), tile(g)], vn_ref[blk(b1), tile(g)]], axis=1)
        mixed = _dot(wtril_ref[g], rhs) + bs_ref[:, g:g + 1]
        a_ref[blk(b0), tile(g)] = (u_ref[blk(b0), tile(g)].astype(F32)
                                   * mixed[:, :LANES]).astype(BF16)
        a_ref[blk(b1), tile(g)] = (u_ref[blk(b1), tile(g)].astype(F32)
                                   * mixed[:, LANES:]).astype(BF16)

    first = ((i * MIX_BLOCKS) % blocks_per_seq) == 0
    prev_bias = jnp.where(first, NEG_INF, 0.0).astype(F32)
    low = col < HEAD_DIM
    ones = jnp.ones((2 * CHUNK, LANES), BF16)

    def band(ref, prev_ref, b, g):
        prev = prev_ref[:, tile(g)] if b == 0 else ref[blk(b - 1), tile(g)]
        return jnp.concatenate([prev, ref[blk(b), tile(g)]], axis=0)

    def scores(b, g):
        q_tiles = []
        for t in range(Q_PER_KV // 2):
            qt = q_ref[blk(b), tile(2 * g + t)]
            zero = jnp.zeros_like(qt)
            q_tiles += [jnp.where(low, qt, zero), jnp.where(low, zero, qt)]
        q4 = jnp.concatenate(q_tiles, axis=0)
        return lax.dot_general(q4, band(kd_ref, kdp_ref, b, g), (((1,), (1,)), ((), ())),
                               preferred_element_type=F32)

    def sink2(head):
        return sinks_ref[head] * LOG2E

    def attend(b, g, s4):
        p_rows, maxes = [], []
        for hh in range(Q_PER_KV):
            s_prev = s4[blk(hh), :CHUNK]
            if b == 0:
                s_prev = s_prev + prev_bias
            s = jnp.where(causal, s4[blk(hh), CHUNK:], s_prev)
            m = jnp.maximum(jnp.max(s, axis=-1, keepdims=True), sink2(Q_PER_KV * g + hh))
            p = jnp.exp2(s - m)
            pz = jnp.zeros_like(p)
            p_rows.append(jnp.concatenate(
                [jnp.where(causal, pz, p), jnp.where(causal, p, pz)], axis=1).astype(BF16))
            maxes.append(m)
        pband = jnp.concatenate(p_rows, axis=0)
        vband = jnp.concatenate([band(vd_ref, vdp_ref, b, g), ones], axis=1)
        o4 = _dot(pband, vband)
        for t in range(Q_PER_KV // 2):
            ha, hb = 2 * t, 2 * t + 1
            sink = jnp.where(low[:1], sink2(Q_PER_KV * g + ha), sink2(Q_PER_KV * g + hb))
            m_ab = jnp.where(low, maxes[ha], maxes[hb])
            denom = jnp.where(low, o4[blk(ha), LANES:], o4[blk(hb), LANES:]) + jnp.exp2(sink - m_ab)
            o_ab = jnp.where(low, o4[blk(ha), :LANES], o4[blk(hb), :LANES])
            att_ref[blk(b), tile(2 * g + t)] = (o_ab * (1.0 / denom)).astype(BF16)

    for g in range(SGU_GROUPS):
        for bp in range(MIX_BLOCKS // 2):
            sgu_piece(g, bp)
    for b in range(MIX_BLOCKS):
        for g in range(N_KV_HEADS):
            attend(b, g, scores(b, g))


def _mixing(sinks, u, vn, q, kd, vd, w_spatial, b_spatial_t, blocks_per_seq):
    tokens = u.shape[0]
    rows = MIX_BLOCKS * CHUNK
    grid = (tokens // rows,)
    blk = lambda width: pl.BlockSpec((rows, width), lambda i: (i, 0))
    prev = pl.BlockSpec((CHUNK, KV_DUP_WIDTH),
                        lambda i: (jnp.maximum(i * MIX_BLOCKS - 1, 0), 0))
    return pl.pallas_call(
        functools.partial(_mix_kernel, blocks_per_seq),
        grid=grid,
        in_specs=[
            pl.BlockSpec(memory_space=pltpu.SMEM),
            blk(D_MODEL), blk(D_MODEL), blk(D_MODEL), blk(KV_DUP_WIDTH), blk(KV_DUP_WIDTH),
            prev, prev,
            pl.BlockSpec((SGU_GROUPS, CHUNK, CHUNK), lambda i: (0, 0, 0)),
            pl.BlockSpec((CHUNK, SGU_GROUPS), lambda i: (0, 0)),
        ],
        out_specs=[blk(D_MODEL), blk(D_MODEL)],
        out_shape=[jax.ShapeDtypeStruct((tokens, D_MODEL), BF16)] * 2,
        scratch_shapes=[pltpu.VMEM((SGU_GROUPS, CHUNK, CHUNK), BF16)],
        compiler_params=pltpu.CompilerParams(
            dimension_semantics=("arbitrary",), vmem_limit_bytes=VMEM_LIMIT_BYTES),
        name="mix",
    )(sinks, u, vn, q, kd, vd, kd, vd, w_spatial, b_spatial_t)


def _out_kernel(x_ref, a_ref, att_ref, ga_ref, gb_ref, wa_ref, wb_ref, wo_ref,
                gpost_ref, wf1_ref, wf2_ref, gffpre_ref, gffpost_ref, o_ref, h1_ref):
    subs = [slice(t * SUB_ROWS, (t + 1) * SUB_ROWS) for t in range(x_ref.shape[0] // SUB_ROWS)]
    merged = [(ga_ref[r, :].astype(F32) * _dot(a_ref[r, :], wa_ref[...])
               + gb_ref[r, :].astype(F32) * _dot(att_ref[r, :], wb_ref[...])).astype(BF16)
              for r in subs]
    mix = [_dot(m, wo_ref[...]) for m in merged]
    x1 = [x_ref[r, :] + _rms(mx, gpost_ref[...]) for r, mx in zip(subs, mix)]
    hf = [_rms(x, gffpre_ref[...]).astype(BF16) for x in x1]
    for c in range(D_FF // FF_CHUNK):
        sl = slice(c * FF_CHUNK, (c + 1) * FF_CHUNK)
        for r, h in zip(subs, hf):
            h1 = jnp.maximum(_dot(h, wf1_ref[:, sl]), 0.0)
            h1_ref[r, sl] = (h1 * h1).astype(BF16)
    ff = [_dot(h1_ref[r, :], wf2_ref[...]) for r in subs]
    for r, x, f in zip(subs, x1, ff):
        o_ref[r, :] = x + _rms(f, gffpost_ref[...])


def _output(x2, a, att, ga, gb, wa, wb, wo, gpost, wf1, wf2, gffpre, gffpost):
    tokens = x2.shape[0]
    rows = OUT_ROWS
    grid = (tokens // rows,)
    row_spec = pl.BlockSpec((rows, D_MODEL), lambda i: (i, 0))
    const = lambda shape: pl.BlockSpec(shape, lambda i: (0,) * len(shape),
                                       pipeline_mode=pl.Buffered(1))
    return pl.pallas_call(
        _out_kernel,
        grid=grid,
        in_specs=[
            row_spec, row_spec, row_spec, row_spec, row_spec,
            const((D_MODEL, D_MODEL)), const((D_MODEL, D_MODEL)), const((D_MODEL, D_MODEL)),
            const((1, D_MODEL)),
            const((D_MODEL, D_FF)), const((D_FF, D_MODEL)),
            const((1, D_MODEL)), const((1, D_MODEL)),
        ],
        out_specs=row_spec,
        out_shape=jax.ShapeDtypeStruct((tokens, D_MODEL), F32),
        scratch_shapes=[pltpu.VMEM((rows, D_FF), BF16)],
        compiler_params=pltpu.CompilerParams(
            dimension_semantics=("arbitrary",), vmem_limit_bytes=VMEM_LIMIT_BYTES),
        name="out",
    )(x2, a, att, ga, gb, wa, wb, wo, gpost, wf1, wf2, gffpre, gffpost)


def _rope_inv_freq_rows():
    inv_freq = ROPE_THETA ** (-jnp.arange(0, ROPE_DIM, 2, dtype=F32) / ROPE_DIM)
    return jnp.broadcast_to(inv_freq[:, None], (ROPE_HALF, LANES))


def kernel(x, positions, w_in, ln_v_gain, ln_v_bias, w_spatial, b_spatial, sinks, w_a, w_b, w_o, norm_mix_pre, norm_mix_post, w_ff_in, w_ff_out, norm_ff_pre, norm_ff_post):
    batch, seq, _ = x.shape
    depth = w_in.shape[0]
    tokens = batch * seq
    row = lambda v: v.reshape(1, -1)
    invf = _rope_inv_freq_rows()
    pos3 = positions.reshape(tokens // PROJ_ROWS, PROJ_ROWS // CHUNK, LANES)
    x2 = x.reshape(tokens, D_MODEL)
    for l in range(depth):
        u, vn, q, kd, vd, ga, gb, wa, wb, wo, wf1, wf2 = _projection(
            x2, pos3, invf, row(norm_mix_pre[l]), w_in[l].astype(BF16),
            row(ln_v_gain[l]), row(ln_v_bias[l]),
            (w_a[l], w_b[l], w_o[l], w_ff_in[l], w_ff_out[l]))
        a, att = _mixing(sinks[l], u, vn, q, kd, vd, w_spatial[l], b_spatial[l].T,
                         seq // CHUNK)
        x2 = _output(x2, a, att, ga, gb, wa, wb, wo, row(norm_mix_post[l]), wf1, wf2,
                     row(norm_ff_pre[l]), row(norm_ff_post[l]))
    return x2.reshape(batch, seq, D_MODEL)
```

```python
import functools
import math

import jax
import jax.numpy as jnp
import numpy as np
from jax import lax
from jax.experimental import pallas as pl
from jax.experimental.pallas import tpu as pltpu

D_MODEL = 1024
SGU_GROUPS = 8
CHUNK = 128
HEAD_DIM = 64
N_Q_HEADS = 16
N_KV_HEADS = 4
Q_PER_KV = N_Q_HEADS // N_KV_HEADS
KV_WIDTH = N_KV_HEADS * HEAD_DIM
ROPE_DIM = 16
ROPE_HALF = ROPE_DIM // 2
ROPE_THETA = 500000.0
D_FF = 4 * D_MODEL
EPS = 1e-6
LANES = 128
SUBLANES = 8
NEG_INF = -1e30
KV_DUP_WIDTH = N_KV_HEADS * LANES

_OFF_U = 0
_OFF_V = _OFF_U + D_MODEL
_OFF_Q = _OFF_V + D_MODEL
_OFF_K = _OFF_Q + D_MODEL
_OFF_VA = _OFF_K + KV_WIDTH
_OFF_GA = _OFF_VA + KV_WIDTH
_OFF_GB = _OFF_GA + D_MODEL
IN_WIDTH = _OFF_GB + D_MODEL

VMEM_LIMIT_BYTES = 56 * 1024 * 1024
PROJ_ROWS = 512
MIX_BLOCKS = 8
OUT_ROWS = 512
SUB_ROWS = 256
FF_CHUNK = 1024

BF16 = jnp.bfloat16
F32 = jnp.float32


def _rms(x, gain):
    return x * lax.rsqrt(jnp.mean(x * x, axis=-1, keepdims=True) + EPS) * gain


def _dot(a, b):
    return jnp.dot(a, b, preferred_element_type=F32)


LOG2E = math.log2(math.e)


def _sigmoid(x):
    return 1.0 / (1.0 + jnp.exp2(x * -LOG2E))


_GELU_K1 = -2.0 * math.sqrt(2.0 / math.pi) * math.log2(math.e)
_GELU_K3 = _GELU_K1 * 0.044715


def _gelu_tanh(x):
    return x / (1.0 + jnp.exp2(x * (x * x * _GELU_K3 + _GELU_K1)))


def _low_half_lanes(rows):
    return lax.broadcasted_iota(jnp.int32, (rows, LANES), 1) < HEAD_DIM


def _rope_tables(pos_row, invf):
    ang = invf * pos_row
    cos = jnp.cos(ang)
    sin = jnp.sin(ang)
    pad = HEAD_DIM - ROPE_DIM
    ones = jnp.ones((pad, LANES), F32)
    zeros = jnp.zeros((pad, LANES), F32)
    cos_lt = jnp.concatenate([cos, cos, ones] * (LANES // HEAD_DIM), axis=0)
    sin_lt = jnp.concatenate([-sin, sin, zeros] * (LANES // HEAD_DIM), axis=0)
    return cos_lt.T, sin_lt.T


_W_SEGMENTS = ((_OFF_Q, D_MODEL), (_OFF_K, 2 * KV_WIDTH), (_OFF_V, D_MODEL),
               (_OFF_U, D_MODEL), (_OFF_GA, D_MODEL), (_OFF_GB, D_MODEL))
_W_STAGE_SLOTS = 2
_N_PROJ_OUTS = 7
_N_PROJ_SCRATCH = 3


def _proj_kernel(x_ref, pos_ref, invf_ref, gpre_ref, w_hbm, lng_ref, lnb_ref, *refs):
    args = (x_ref, pos_ref, invf_ref, gpre_ref, w_hbm, lng_ref, lnb_ref) + refs
    pl.when(pl.program_id(0) == 0)(functools.partial(_proj_body, True, *args))
    pl.when(pl.program_id(0) != 0)(functools.partial(_proj_body, False, *args))


def _proj_body(first, x_ref, pos_ref, invf_ref, gpre_ref, w_hbm, lng_ref, lnb_ref, *refs):
    n_side = len(refs) - _N_PROJ_OUTS - _N_PROJ_SCRATCH >> 1
    side_in, refs = refs[:n_side], refs[n_side:]
    outs, refs = refs[:_N_PROJ_OUTS], refs[_N_PROJ_OUTS:]
    side_out, (w_ref, stage_ref, sem) = refs[:n_side], refs[n_side:]
    u_ref, vn_ref, q_ref, kd_ref, vd_ref, ga_ref, gb_ref = outs

    def segment_copy(k):
        lo, width = _W_SEGMENTS[k]
        slot = k % _W_STAGE_SLOTS
        return pltpu.make_async_copy(w_hbm.at[:, pl.ds(lo, width)],
                                     stage_ref.at[slot, :, pl.ds(0, width)], sem.at[slot])

    def load_segment(k):
        if not first:
            return
        lo, width = _W_SEGMENTS[k]
        segment_copy(k).wait()
        w_ref[:, lo:lo + width] = stage_ref[k % _W_STAGE_SLOTS, :, :width].astype(BF16)
        if k + _W_STAGE_SLOTS < len(_W_SEGMENTS):
            segment_copy(k + _W_STAGE_SLOTS).start()

    if first:
        for k in range(_W_STAGE_SLOTS):
            segment_copy(k).start()

    for src, dst in zip(side_in, side_out):
        dst[...] = src[...].astype(BF16)
    rows = x_ref.shape[0]
    subs = [slice(t * SUB_ROWS, (t + 1) * SUB_ROWS) for t in range(rows // SUB_ROWS)]
    hs = [_rms(x_ref[r, :], gpre_ref[...]).astype(BF16) for r in subs]

    def proj(h, lo, width):
        return _dot(h, w_ref[:, lo:lo + width])

    low = _low_half_lanes(CHUNK)
    d = lax.broadcasted_iota(jnp.int32, (CHUNK, LANES), 1) & (HEAD_DIM - 1)
    pair_next = d < ROPE_HALF
    chunks = [slice(j * CHUNK, (j + 1) * CHUNK) for j in range(rows // CHUNK)]
    tile = lambda c: slice(c * LANES, (c + 1) * LANES)
    tables = [_rope_tables(pos_ref[0, j:j + 1, :].astype(F32), invf_ref[...])
              for j in range(rows // CHUNK)]

    def rot(x, c_t, s_t):
        partner = jnp.where(pair_next, pltpu.roll(x, LANES - ROPE_HALF, axis=1),
                            pltpu.roll(x, ROPE_HALF, axis=1))
        return x * c_t + partner * s_t

    def store_dup(ref, rc, c, t):
        rolled = pltpu.roll(t, HEAD_DIM, axis=1)
        ref[rc, tile(2 * c)] = jnp.where(low, t, rolled).astype(BF16)
        ref[rc, tile(2 * c + 1)] = jnp.where(low, rolled, t).astype(BF16)

    scale = HEAD_DIM ** -0.5 * LOG2E
    load_segment(0)
    q = jnp.concatenate([proj(h, _OFF_Q, D_MODEL) for h in hs], axis=0)
    for rc, (c_tab, s_tab) in zip(chunks, tables):
        cq, sq = c_tab * scale, s_tab * scale
        for c in range(D_MODEL // LANES):
            q_ref[rc, tile(c)] = rot(q[rc, tile(c)], cq, sq).astype(BF16)
    load_segment(1)
    k = jnp.concatenate([proj(h, _OFF_K, KV_WIDTH) for h in hs], axis=0)
    for rc, (c_tab, s_tab) in zip(chunks, tables):
        for c in range(KV_WIDTH // LANES):
            store_dup(kd_ref, rc, c, rot(k[rc, tile(c)], c_tab, s_tab))

    load_segment(2)
    for r, h in zip(subs, hs):
        v = _gelu_tanh(proj(h, _OFF_V, D_MODEL))
        mu = jnp.mean(v, axis=-1, keepdims=True)
        vc = v - mu
        var = jnp.mean(vc * vc, axis=-1, keepdims=True)
        vn_ref[r, :] = (vc * lax.rsqrt(var + EPS) * lng_ref[...] + lnb_ref[...]).astype(BF16)
    load_segment(3)
    for r, h in zip(subs, hs):
        u_ref[r, :] = _gelu_tanh(proj(h, _OFF_U, D_MODEL)).astype(BF16)
    load_segment(4)
    for r, h in zip(subs, hs):
        ga_ref[r, :] = _sigmoid(proj(h, _OFF_GA, D_MODEL)).astype(BF16)
    load_segment(5)
    for r, h in zip(subs, hs):
        gb_ref[r, :] = _sigmoid(proj(h, _OFF_GB, D_MODEL)).astype(BF16)
    va = jnp.concatenate([proj(h, _OFF_VA, KV_WIDTH) for h in hs], axis=0)
    for rc in chunks:
        for c in range(KV_WIDTH // LANES):
            store_dup(vd_ref, rc, c, va[rc, tile(c)])


def _projection(x2, pos3, invf, gpre, w_in, lng, lnb, side_weights):
    tokens = x2.shape[0]
    rows = PROJ_ROWS
    steps = tokens // rows
    row_spec = lambda width: pl.BlockSpec((rows, width), lambda i: (i, 0))
    full = lambda shape: pl.BlockSpec(shape, lambda i: (0,) * len(shape))
    slab = lambda w: pl.BlockSpec((w.shape[0] // steps, w.shape[1]), lambda i: (i, 0))
    out_widths = (D_MODEL, D_MODEL, D_MODEL, KV_DUP_WIDTH, KV_DUP_WIDTH, D_MODEL, D_MODEL)
    return pl.pallas_call(
        _proj_kernel,
        grid=(steps,),
        in_specs=[
            row_spec(D_MODEL),
            pl.BlockSpec((1, rows // CHUNK, LANES), lambda i: (i, 0, 0)),
            full((SUBLANES, LANES)),
            full((1, D_MODEL)),
            pl.BlockSpec(memory_space=pl.ANY),
            full((1, D_MODEL)),
            full((1, D_MODEL)),
        ] + [slab(w) for w in side_weights],
        out_specs=[row_spec(w) for w in out_widths] + [slab(w) for w in side_weights],
        out_shape=([jax.ShapeDtypeStruct((tokens, w), BF16) for w in out_widths]
                   + [jax.ShapeDtypeStruct(w.shape, BF16) for w in side_weights]),
        scratch_shapes=[
            pltpu.VMEM((D_MODEL, IN_WIDTH), BF16),
            pltpu.VMEM((_W_STAGE_SLOTS, D_MODEL, D_MODEL), F32),
            pltpu.SemaphoreType.DMA((_W_STAGE_SLOTS,)),
        ],
        compiler_params=pltpu.CompilerParams(
            dimension_semantics=("arbitrary",), vmem_limit_bytes=VMEM_LIMIT_BYTES),
        name="proj",
    )(x2, pos3, invf, gpre, w_in, lng, lnb, *side_weights)


def _mix_kernel(blocks_per_seq, sinks_ref, u_ref, vn_ref, q_ref, kd_ref, vd_ref,
                kdp_ref, vdp_ref, ws_ref, bs_ref, a_ref, att_ref, wtril_ref):
    i = pl.program_id(0)
    row = lax.broadcasted_iota(jnp.int32, (CHUNK, CHUNK), 0)
    col = lax.broadcasted_iota(jnp.int32, (CHUNK, CHUNK), 1)
    causal = col <= row

    @pl.when(i == 0)
    def _():
        for g in range(SGU_GROUPS):
            wtril_ref[g] = jnp.where(causal, ws_ref[g], 0.0).astype(BF16)

    blk = lambda b: slice(b * CHUNK, (b + 1) * CHUNK)
    tile = lambda c: slice(c * LANES, (c + 1) * LANES)

    def sgu_piece(g, bp):
        b0, b1 = 2 * bp, 2 * bp + 1
        rhs = jnp.concatenate([vn_ref[blk(b0), tile(g)], vn_ref[blk(b1), tile(g)]], axis=1)
        mixed = _dot(wtril_ref[g], rhs) + bs_ref[:, g:g + 1]
        a_ref[blk(b0), tile(g)] = (u_ref[blk(b0), tile(g)].astype(F32)
                                   * mixed[:, :LANES]).astype(BF16)
        a_ref[blk(b1), tile(g)] = (u_ref[blk(b1), tile(g)].astype(F32)
                                   * mixed[:, LANES:]).astype(BF16)

    first = ((i * MIX_BLOCKS) % blocks_per_seq) == 0
    prev_bias = jnp.where(first, NEG_INF, 0.0).astype(F32)
    low = col < HEAD_DIM
    ones = jnp.ones((2 * CHUNK, LANES), BF16)

    def band(ref, prev_ref, b, g):
        prev = prev_ref[:, tile(g)] if b == 0 else ref[blk(b - 1), tile(g)]
        return jnp.concatenate([prev, ref[blk(b), tile(g)]], axis=0)

    def scores(b, g):
        q_tiles = []
        for t in range(Q_PER_KV // 2):
            qt = q_ref[blk(b), tile(2 * g + t)]
            zero = jnp.zeros_like(qt)
            q_tiles += [jnp.where(low, qt, zero), jnp.where(low, zero, qt)]
        q4 = jnp.concatenate(q_tiles, axis=0)
        return lax.dot_general(q4, band(kd_ref, kdp_ref, b, g), (((1,), (1,)), ((), ())),
                               preferred_element_type=F32)

    def sink2(head):
        return sinks_ref[head] * LOG2E

    def attend(b, g, s4):
        p_rows, maxes = [], []
        for hh in range(Q_PER_KV):
            s_prev = s4[blk(hh), :CHUNK]
            if b == 0:
                s_prev = s_prev + prev_bias
            s = jnp.where(causal, s4[blk(hh), CHUNK:], s_prev)
            m = jnp.maximum(jnp.max(s, axis=-1, keepdims=True), sink2(Q_PER_KV * g + hh))
            p = jnp.exp2(s - m)
            pz = jnp.zeros_like(p)
            p_rows.append(jnp.concatenate(
                [jnp.where(causal, pz, p), jnp.where(causal, p, pz)], axis=1).astype(BF16))
            maxes.append(m)
        pband = jnp.concatenate(p_rows, axis=0)
        vband = jnp.concatenate([band(vd_ref, vdp_ref, b, g), ones], axis=1)
        o4 = _dot(pband, vband)
        for t in range(Q_PER_KV // 2):
            ha, hb = 2 * t, 2 * t + 1
            sink = jnp.where(low[:1], sink2(Q_PER_KV * g + ha), sink2(Q_PER_KV * g + hb))
            m_ab = jnp.where(low, maxes[ha], maxes[hb])
            denom = jnp.where(low, o4[blk(ha), LANES:], o4[blk(hb), LANES:]) + jnp.exp2(sink - m_ab)
            o_ab = jnp.where(low, o4[blk(ha), :LANES], o4[blk(hb), :LANES])
            att_ref[blk(b), tile(2 * g + t)] = (o_ab * (1.0 / denom)).astype(BF16)

    for g in range(SGU_GROUPS):
        for bp in range(MIX_BLOCKS // 2):
            sgu_piece(g, bp)
    for b in range(MIX_BLOCKS):
        for g in range(N_KV_HEADS):
            attend(b, g, scores(b, g))


def _mixing(sinks, u, vn, q, kd, vd, w_spatial, b_spatial_t, blocks_per_seq):
    tokens = u.shape[0]
    rows = MIX_BLOCKS * CHUNK
    grid = (tokens // rows,)
    blk = lambda width: pl.BlockSpec((rows, width), lambda i: (i, 0))
    prev = pl.BlockSpec((CHUNK, KV_DUP_WIDTH),
                        lambda i: (jnp.maximum(i * MIX_BLOCKS - 1, 0), 0))
    return pl.pallas_call(
        functools.partial(_mix_kernel, blocks_per_seq),
        grid=grid,
        in_specs=[
            pl.BlockSpec(memory_space=pltpu.SMEM),
            blk(D_MODEL), blk(D_MODEL), blk(D_MODEL), blk(KV_DUP_WIDTH), blk(KV_DUP_WIDTH),
            prev, prev,
            pl.BlockSpec((SGU_GROUPS, CHUNK, CHUNK), lambda i: (0, 0, 0)),
            pl.BlockSpec((CHUNK, SGU_GROUPS), lambda i: (0, 0)),
        ],
        out_specs=[blk(D_MODEL), blk(D_MODEL)],
        out_shape=[jax.ShapeDtypeStruct((tokens, D_MODEL), BF16)] * 2,
        scratch_shapes=[pltpu.VMEM((SGU_GROUPS, CHUNK, CHUNK), BF16)],
        compiler_params=pltpu.CompilerParams(
            dimension_semantics=("arbitrary",), vmem_limit_bytes=VMEM_LIMIT_BYTES),
        name="mix",
    )(sinks, u, vn, q, kd, vd, kd, vd, w_spatial, b_spatial_t)


_N_FF_CHUNKS = D_FF // FF_CHUNK
_N_OUT_WEIGHT_COPIES = 4 + _N_FF_CHUNKS


def _out_kernel(*refs):
    pl.when(pl.program_id(0) == 0)(functools.partial(_out_body, True, *refs))
    pl.when(pl.program_id(0) != 0)(functools.partial(_out_body, False, *refs))


def _out_body(first, x_ref, a_ref, att_ref, ga_ref, gb_ref, wa_hbm, wb_hbm, wo_hbm,
              gpost_ref, wf1_hbm, wf2_hbm, gffpre_ref, gffpost_ref, o_ref,
              h1_ref, wa_ref, wb_ref, wo_ref, wf1_ref, wf2_ref, sem):
    ff_cols = [slice(c * FF_CHUNK, (c + 1) * FF_CHUNK) for c in range(_N_FF_CHUNKS)]
    copies = ([(wa_hbm, wa_ref), (wb_hbm, wb_ref), (wo_hbm, wo_ref)]
              + [(wf1_hbm.at[:, sl], wf1_ref.at[:, sl]) for sl in ff_cols]
              + [(wf2_hbm, wf2_ref)])
    copies = [pltpu.make_async_copy(src, dst, sem.at[n]) for n, (src, dst) in enumerate(copies)]

    def ready(n):
        if first:
            copies[n].wait()

    if first:
        for copy in copies:
            copy.start()

    subs = [slice(t * SUB_ROWS, (t + 1) * SUB_ROWS) for t in range(x_ref.shape[0] // SUB_ROWS)]
    ready(0)
    ready(1)
    merged = [(ga_ref[r, :].astype(F32) * _dot(a_ref[r, :], wa_ref[...])
               + gb_ref[r, :].astype(F32) * _dot(att_ref[r, :], wb_ref[...])).astype(BF16)
              for r in subs]
    ready(2)
    mix = [_dot(m, wo_ref[...]) for m in merged]
    x1 = [x_ref[r, :] + _rms(mx, gpost_ref[...]) for r, mx in zip(subs, mix)]
    hf = [_rms(x, gffpre_ref[...]).astype(BF16) for x in x1]
    for c, sl in enumerate(ff_cols):
        ready(3 + c)
        for r, h in zip(subs, hf):
            h1 = jnp.maximum(_dot(h, wf1_ref[:, sl]), 0.0)
            h1_ref[r, sl] = (h1 * h1).astype(BF16)
    ready(3 + _N_FF_CHUNKS)
    ff = [_dot(h1_ref[r, :], wf2_ref[...]) for r in subs]
    for r, x, f in zip(subs, x1, ff):
        o_ref[r, :] = x + _rms(f, gffpost_ref[...])


def _output(x2, a, att, ga, gb, wa, wb, wo, gpost, wf1, wf2, gffpre, gffpost):
    tokens = x2.shape[0]
    rows = OUT_ROWS
    grid = (tokens // rows,)
    row_spec = pl.BlockSpec((rows, D_MODEL), lambda i: (i, 0))
    const = lambda shape: pl.BlockSpec(shape, lambda i: (0,) * len(shape),
                                       pipeline_mode=pl.Buffered(1))
    hbm = pl.BlockSpec(memory_space=pl.ANY)
    return pl.pallas_call(
        _out_kernel,
        grid=grid,
        in_specs=[
            row_spec, row_spec, row_spec, row_spec, row_spec,
            hbm, hbm, hbm,
            const((1, D_MODEL)),
            hbm, hbm,
            const((1, D_MODEL)), const((1, D_MODEL)),
        ],
        out_specs=row_spec,
        out_shape=jax.ShapeDtypeStruct((tokens, D_MODEL), F32),
        scratch_shapes=[
            pltpu.VMEM((rows, D_FF), BF16),
            pltpu.VMEM((D_MODEL, D_MODEL), BF16), pltpu.VMEM((D_MODEL, D_MODEL), BF16),
            pltpu.VMEM((D_MODEL, D_MODEL), BF16),
            pltpu.VMEM((D_MODEL, D_FF), BF16), pltpu.VMEM((D_FF, D_MODEL), BF16),
            pltpu.SemaphoreType.DMA((_N_OUT_WEIGHT_COPIES,)),
        ],
        compiler_params=pltpu.CompilerParams(
            dimension_semantics=("arbitrary",), vmem_limit_bytes=VMEM_LIMIT_BYTES),
        name="out",
    )(x2, a, att, ga, gb, wa, wb, wo, gpost, wf1, wf2, gffpre, gffpost)


def _rope_inv_freq_rows():
    inv_freq = ROPE_THETA ** (-jnp.arange(0, ROPE_DIM, 2, dtype=F32) / ROPE_DIM)
    return jnp.broadcast_to(inv_freq[:, None], (ROPE_HALF, LANES))


def kernel(x, positions, w_in, ln_v_gain, ln_v_bias, w_spatial, b_spatial, sinks, w_a, w_b, w_o, norm_mix_pre, norm_mix_post, w_ff_in, w_ff_out, norm_ff_pre, norm_ff_post):
    batch, seq, _ = x.shape
    depth = w_in.shape[0]
    tokens = batch * seq
    row = lambda v: v.reshape(1, -1)
    invf = _rope_inv_freq_rows()
    pos3 = positions.reshape(tokens // PROJ_ROWS, PROJ_ROWS // CHUNK, LANES)
    x2 = x.reshape(tokens, D_MODEL)
    for l in range(depth):
        u, vn, q, kd, vd, ga, gb, wa, wb, wo, wf1, wf2 = _projection(
            x2, pos3, invf, row(norm_mix_pre[l]), w_in[l],
            row(ln_v_gain[l]), row(ln_v_bias[l]),
            (w_a[l], w_b[l], w_o[l], w_ff_in[l], w_ff_out[l]))
        a, att = _mixing(sinks[l], u, vn, q, kd, vd, w_spatial[l], b_spatial[l].T,
                         seq // CHUNK)
        x2 = _output(x2, a, att, ga, gb, wa, wb, wo, row(norm_mix_post[l]), wf1, wf2,
                     row(norm_ff_pre[l]), row(norm_ff_post[l]))
    return x2.reshape(batch, seq, D_MODEL)
```

```python
import functools
import math

import jax
import jax.numpy as jnp
import numpy as np
from jax import lax
from jax.experimental import pallas as pl
from jax.experimental.pallas import tpu as pltpu

D_MODEL = 1024
SGU_GROUPS = 8
CHUNK = 128
HEAD_DIM = 64
N_Q_HEADS = 16
N_KV_HEADS = 4
Q_PER_KV = N_Q_HEADS // N_KV_HEADS
KV_WIDTH = N_KV_HEADS * HEAD_DIM
ROPE_DIM = 16
ROPE_HALF = ROPE_DIM // 2
ROPE_THETA = 500000.0
D_FF = 4 * D_MODEL
EPS = 1e-6
LANES = 128
SUBLANES = 8
NEG_INF = -1e30
KV_DUP_WIDTH = N_KV_HEADS * LANES

_OFF_U = 0
_OFF_V = _OFF_U + D_MODEL
_OFF_Q = _OFF_V + D_MODEL
_OFF_K = _OFF_Q + D_MODEL
_OFF_VA = _OFF_K + KV_WIDTH
_OFF_GA = _OFF_VA + KV_WIDTH
_OFF_GB = _OFF_GA + D_MODEL
IN_WIDTH = _OFF_GB + D_MODEL

VMEM_LIMIT_BYTES = 56 * 1024 * 1024
PROJ_ROWS = 512
MIX_BLOCKS = 8
OUT_ROWS = 512
SUB_ROWS = 256
FF_CHUNK = 1024

BF16 = jnp.bfloat16
F32 = jnp.float32


def _rms(x, gain):
    return x * lax.rsqrt(jnp.mean(x * x, axis=-1, keepdims=True) + EPS) * gain


def _dot(a, b):
    return jnp.dot(a, b, preferred_element_type=F32)


LOG2E = math.log2(math.e)


def _sigmoid(x):
    return 1.0 / (1.0 + jnp.exp2(x * -LOG2E))


_GELU_K1 = -2.0 * math.sqrt(2.0 / math.pi) * math.log2(math.e)
_GELU_K3 = _GELU_K1 * 0.044715


def _gelu_tanh(x):
    return x / (1.0 + jnp.exp2(x * (x * x * _GELU_K3 + _GELU_K1)))


def _low_half_lanes(rows):
    return lax.broadcasted_iota(jnp.int32, (rows, LANES), 1) < HEAD_DIM


def _rope_tables(pos_row, invf):
    ang = invf * pos_row
    cos = jnp.cos(ang)
    sin = jnp.sin(ang)
    pad = HEAD_DIM - ROPE_DIM
    ones = jnp.ones((pad, LANES), F32)
    zeros = jnp.zeros((pad, LANES), F32)
    cos_lt = jnp.concatenate([cos, cos, ones] * (LANES // HEAD_DIM), axis=0)
    sin_lt = jnp.concatenate([-sin, sin, zeros] * (LANES // HEAD_DIM), axis=0)
    return cos_lt.T, sin_lt.T


_W_SEGMENTS = ((_OFF_Q, D_MODEL), (_OFF_K, 2 * KV_WIDTH), (_OFF_V, D_MODEL),
               (_OFF_U, D_MODEL), (_OFF_GA, D_MODEL), (_OFF_GB, D_MODEL))
_W_STAGE_SLOTS = 2
_N_PROJ_OUTS = 7
_N_PROJ_SCRATCH = 3


def _proj_kernel(x_ref, pos_ref, invf_ref, gpre_ref, w_hbm, lng_ref, lnb_ref, *refs):
    args = (x_ref, pos_ref, invf_ref, gpre_ref, w_hbm, lng_ref, lnb_ref) + refs
    pl.when(pl.program_id(0) == 0)(functools.partial(_proj_body, True, *args))
    pl.when(pl.program_id(0) != 0)(functools.partial(_proj_body, False, *args))


def _proj_body(first, x_ref, pos_ref, invf_ref, gpre_ref, w_hbm, lng_ref, lnb_ref, *refs):
    n_side = len(refs) - _N_PROJ_OUTS - _N_PROJ_SCRATCH >> 1
    side_in, refs = refs[:n_side], refs[n_side:]
    outs, refs = refs[:_N_PROJ_OUTS], refs[_N_PROJ_OUTS:]
    side_out, (w_ref, stage_ref, sem) = refs[:n_side], refs[n_side:]
    u_ref, vn_ref, q_ref, kd_ref, vd_ref, ga_ref, gb_ref = outs

    def segment_copy(k):
        lo, width = _W_SEGMENTS[k]
        slot = k % _W_STAGE_SLOTS
        return pltpu.make_async_copy(w_hbm.at[:, pl.ds(lo, width)],
                                     stage_ref.at[slot, :, pl.ds(0, width)], sem.at[slot])

    def load_segment(k):
        if not first:
            return
        lo, width = _W_SEGMENTS[k]
        segment_copy(k).wait()
        w_ref[:, lo:lo + width] = stage_ref[k % _W_STAGE_SLOTS, :, :width].astype(BF16)
        if k + _W_STAGE_SLOTS < len(_W_SEGMENTS):
            segment_copy(k + _W_STAGE_SLOTS).start()

    if first:
        for k in range(_W_STAGE_SLOTS):
            segment_copy(k).start()

    for src, dst in zip(side_in, side_out):
        dst[...] = src[...].astype(BF16)
    rows = x_ref.shape[0]
    subs = [slice(t * SUB_ROWS, (t + 1) * SUB_ROWS) for t in range(rows // SUB_ROWS)]
    hs = [_rms(x_ref[r, :], gpre_ref[...]).astype(BF16) for r in subs]

    def proj(h, lo, width):
        return _dot(h, w_ref[:, lo:lo + width])

    low = _low_half_lanes(CHUNK)
    d = lax.broadcasted_iota(jnp.int32, (CHUNK, LANES), 1) & (HEAD_DIM - 1)
    pair_next = d < ROPE_HALF
    chunks = [slice(j * CHUNK, (j + 1) * CHUNK) for j in range(rows // CHUNK)]
    tile = lambda c: slice(c * LANES, (c + 1) * LANES)
    tables = [_rope_tables(pos_ref[0, j:j + 1, :].astype(F32), invf_ref[...])
              for j in range(rows // CHUNK)]

    def rot(x, c_t, s_t):
        partner = jnp.where(pair_next, pltpu.roll(x, LANES - ROPE_HALF, axis=1),
                            pltpu.roll(x, ROPE_HALF, axis=1))
        return x * c_t + partner * s_t

    def store_dup(ref, rc, c, t):
        rolled = pltpu.roll(t, HEAD_DIM, axis=1)
        ref[rc, tile(2 * c)] = jnp.where(low, t, rolled).astype(BF16)
        ref[rc, tile(2 * c + 1)] = jnp.where(low, rolled, t).astype(BF16)

    scale = HEAD_DIM ** -0.5 * LOG2E
    load_segment(0)
    q = jnp.concatenate([proj(h, _OFF_Q, D_MODEL) for h in hs], axis=0)
    for rc, (c_tab, s_tab) in zip(chunks, tables):
        cq, sq = c_tab * scale, s_tab * scale
        for c in range(D_MODEL // LANES):
            q_ref[rc, tile(c)] = rot(q[rc, tile(c)], cq, sq).astype(BF16)
    load_segment(1)
    k = jnp.concatenate([proj(h, _OFF_K, KV_WIDTH) for h in hs], axis=0)
    for rc, (c_tab, s_tab) in zip(chunks, tables):
        for c in range(KV_WIDTH // LANES):
            store_dup(kd_ref, rc, c, rot(k[rc, tile(c)], c_tab, s_tab))

    load_segment(2)
    for r, h in zip(subs, hs):
        v = _gelu_tanh(proj(h, _OFF_V, D_MODEL))
        mu = jnp.mean(v, axis=-1, keepdims=True)
        vc = v - mu
        var = jnp.mean(vc * vc, axis=-1, keepdims=True)
        vn_ref[r, :] = (vc * lax.rsqrt(var + EPS) * lng_ref[...] + lnb_ref[...]).astype(BF16)
    load_segment(3)
    for r, h in zip(subs, hs):
        u_ref[r, :] = _gelu_tanh(proj(h, _OFF_U, D_MODEL)).astype(BF16)
    load_segment(4)
    for r, h in zip(subs, hs):
        ga_ref[r, :] = _sigmoid(proj(h, _OFF_GA, D_MODEL)).astype(BF16)
    load_segment(5)
    for r, h in zip(subs, hs):
        gb_ref[r, :] = _sigmoid(proj(h, _OFF_GB, D_MODEL)).astype(BF16)
    va = jnp.concatenate([proj(h, _OFF_VA, KV_WIDTH) for h in hs], axis=0)
    for rc in chunks:
        for c in range(KV_WIDTH // LANES):
            store_dup(vd_ref, rc, c, va[rc, tile(c)])


def _projection(x2, pos3, invf, gpre, w_in, lng, lnb, side_weights):
    tokens = x2.shape[0]
    rows = PROJ_ROWS
    steps = tokens // rows
    row_spec = lambda width: pl.BlockSpec((rows, width), lambda i: (i, 0))
    full = lambda shape: pl.BlockSpec(shape, lambda i: (0,) * len(shape))
    slab = lambda w: pl.BlockSpec((w.shape[0] // steps, w.shape[1]), lambda i: (i, 0))
    out_widths = (D_MODEL, D_MODEL, D_MODEL, KV_DUP_WIDTH, KV_DUP_WIDTH, D_MODEL, D_MODEL)
    return pl.pallas_call(
        _proj_kernel,
        grid=(steps,),
        in_specs=[
            row_spec(D_MODEL),
            pl.BlockSpec((1, rows // CHUNK, LANES), lambda i: (i, 0, 0)),
            full((SUBLANES, LANES)),
            full((1, D_MODEL)),
            pl.BlockSpec(memory_space=pl.ANY),
            full((1, D_MODEL)),
            full((1, D_MODEL)),
        ] + [slab(w) for w in side_weights],
        out_specs=[row_spec(w) for w in out_widths] + [slab(w) for w in side_weights],
        out_shape=([jax.ShapeDtypeStruct((tokens, w), BF16) for w in out_widths]
                   + [jax.ShapeDtypeStruct(w.shape, BF16) for w in side_weights]),
        scratch_shapes=[
            pltpu.VMEM((D_MODEL, IN_WIDTH), BF16),
            pltpu.VMEM((_W_STAGE_SLOTS, D_MODEL, D_MODEL), F32),
            pltpu.SemaphoreType.DMA((_W_STAGE_SLOTS,)),
        ],
        compiler_params=pltpu.CompilerParams(
            dimension_semantics=("arbitrary",), vmem_limit_bytes=VMEM_LIMIT_BYTES),
        name="proj",
    )(x2, pos3, invf, gpre, w_in, lng, lnb, *side_weights)


def _mix_kernel(blocks_per_seq, sinks_ref, u_ref, vn_ref, q_ref, kd_ref, vd_ref,
                kdp_ref, vdp_ref, ws_ref, bs_ref, a_ref, att_ref, wtril_ref):
    i = pl.program_id(0)
    row = lax.broadcasted_iota(jnp.int32, (CHUNK, CHUNK), 0)
    col = lax.broadcasted_iota(jnp.int32, (CHUNK, CHUNK), 1)
    causal = col <= row

    @pl.when(i == 0)
    def _():
        for g in range(SGU_GROUPS):
            wtril_ref[g] = jnp.where(causal, ws_ref[g], 0.0).astype(BF16)

    blk = lambda b: slice(b * CHUNK, (b + 1) * CHUNK)
    tile = lambda c: slice(c * LANES, (c + 1) * LANES)

    def sgu_piece(g, bp):
        b0, b1 = 2 * bp, 2 * bp + 1
        rhs = jnp.concatenate([vn_ref[blk(b0), tile(g)], vn_ref[blk(b1), tile(g)]], axis=1)
        mixed = _dot(wtril_ref[g], rhs) + bs_ref[:, g:g + 1]
        a_ref[blk(b0), tile(g)] = (u_ref[blk(b0), tile(g)].astype(F32)
                                   * mixed[:, :LANES]).astype(BF16)
        a_ref[blk(b1), tile(g)] = (u_ref[blk(b1), tile(g)].astype(F32)
                                   * mixed[:, LANES:]).astype(BF16)

    first = ((i * MIX_BLOCKS) % blocks_per_seq) == 0
    prev_bias = jnp.where(first, NEG_INF, 0.0).astype(F32)
    low = col < HEAD_DIM
    ones = jnp.ones((2 * CHUNK, LANES), BF16)

    def band(ref, prev_ref, b, g):
        prev = prev_ref[:, tile(g)] if b == 0 else ref[blk(b - 1), tile(g)]
        return jnp.concatenate([prev, ref[blk(b), tile(g)]], axis=0)

    def scores(b, g):
        q_tiles = []
        for t in range(Q_PER_KV // 2):
            qt = q_ref[blk(b), tile(2 * g + t)]
            zero = jnp.zeros_like(qt)
            q_tiles += [jnp.where(low, qt, zero), jnp.where(low, zero, qt)]
        q4 = jnp.concatenate(q_tiles, axis=0)
        return lax.dot_general(q4, band(kd_ref, kdp_ref, b, g), (((1,), (1,)), ((), ())),
                               preferred_element_type=F32)

    def sink2(head):
        return sinks_ref[head] * LOG2E

    def attend(b, g, s4):
        p_rows, maxes = [], []
        for hh in range(Q_PER_KV):
            s_prev = s4[blk(hh), :CHUNK]
            if b == 0:
                s_prev = s_prev + prev_bias
            s = jnp.where(causal, s4[blk(hh), CHUNK:], s_prev)
            m = jnp.maximum(jnp.max(s, axis=-1, keepdims=True), sink2(Q_PER_KV * g + hh))
            p = jnp.exp2(s - m)
            pz = jnp.zeros_like(p)
            p_rows.append(jnp.concatenate(
                [jnp.where(causal, pz, p), jnp.where(causal, p, pz)], axis=1).astype(BF16))
            maxes.append(m)
        pband = jnp.concatenate(p_rows, axis=0)
        vband = jnp.concatenate([band(vd_ref, vdp_ref, b, g), ones], axis=1)
        o4 = _dot(pband, vband)
        for t in range(Q_PER_KV // 2):
            ha, hb = 2 * t, 2 * t + 1
            sink = jnp.where(low[:1], sink2(Q_PER_KV * g + ha), sink2(Q_PER_KV * g + hb))
            m_ab = jnp.where(low, maxes[ha], maxes[hb])
            denom = jnp.where(low, o4[blk(ha), LANES:], o4[blk(hb), LANES:]) + jnp.exp2(sink - m_ab)
            o_ab = jnp.where(low, o4[blk(ha), :LANES], o4[blk(hb), :LANES])
            att_ref[blk(b), tile(2 * g + t)] = (o_ab * (1.0 / denom)).astype(BF16)

    for g in range(SGU_GROUPS):
        for bp in range(MIX_BLOCKS // 2):
            sgu_piece(g, bp)
    for b in range(MIX_BLOCKS):
        for g in range(N_KV_HEADS):
            attend(b, g, scores(b, g))


def _mixing(sinks, u, vn, q, kd, vd, w_spatial, b_spatial_t, blocks_per_seq):
    tokens = u.shape[0]
    rows = MIX_BLOCKS * CHUNK
    grid = (tokens // rows,)
    blk = lambda width: pl.BlockSpec((rows, width), lambda i: (i, 0))
    prev = pl.BlockSpec((CHUNK, KV_DUP_WIDTH),
                        lambda i: (jnp.maximum(i * MIX_BLOCKS - 1, 0), 0))
    return pl.pallas_call(
        functools.partial(_mix_kernel, blocks_per_seq),
        grid=grid,
        in_specs=[
            pl.BlockSpec(memory_space=pltpu.SMEM),
            blk(D_MODEL), blk(D_MODEL), blk(D_MODEL), blk(KV_DUP_WIDTH), blk(KV_DUP_WIDTH),
            prev, prev,
            pl.BlockSpec((SGU_GROUPS, CHUNK, CHUNK), lambda i: (0, 0, 0)),
            pl.BlockSpec((CHUNK, SGU_GROUPS), lambda i: (0, 0)),
        ],
        out_specs=[blk(D_MODEL), blk(D_MODEL)],
        out_shape=[jax.ShapeDtypeStruct((tokens, D_MODEL), BF16)] * 2,
        scratch_shapes=[pltpu.VMEM((SGU_GROUPS, CHUNK, CHUNK), BF16)],
        compiler_params=pltpu.CompilerParams(
            dimension_semantics=("arbitrary",), vmem_limit_bytes=VMEM_LIMIT_BYTES),
        name="mix",
    )(sinks, u, vn, q, kd, vd, kd, vd, w_spatial, b_spatial_t)


def _out_kernel(x_ref, a_ref, att_ref, ga_ref, gb_ref, wa_ref, wb_ref, wo_ref,
                gpost_ref, wf1_ref, wf2_ref, gffpre_ref, gffpost_ref, o_ref, h1_ref):
    subs = [slice(t * SUB_ROWS, (t + 1) * SUB_ROWS) for t in range(x_ref.shape[0] // SUB_ROWS)]
    merged = [(ga_ref[r, :].astype(F32) * _dot(a_ref[r, :], wa_ref[...])
               + gb_ref[r, :].astype(F32) * _dot(att_ref[r, :], wb_ref[...])).astype(BF16)
              for r in subs]
    mix = [_dot(m, wo_ref[...]) for m in merged]
    x1 = [x_ref[r, :] + _rms(mx, gpost_ref[...]) for r, mx in zip(subs, mix)]
    hf = [_rms(x, gffpre_ref[...]).astype(BF16) for x in x1]
    for c in range(D_FF // FF_CHUNK):
        sl = slice(c * FF_CHUNK, (c + 1) * FF_CHUNK)
        for r, h in zip(subs, hf):
            h1 = jnp.maximum(_dot(h, wf1_ref[:, sl]), 0.0)
            h1_ref[r, sl] = (h1 * h1).astype(BF16)
    ff = [_dot(h1_ref[r, :], wf2_ref[...]) for r in subs]
    for r, x, f in zip(subs, x1, ff):
        o_ref[r, :] = x + _rms(f, gffpost_ref[...])


def _output(x2, a, att, ga, gb, wa, wb, wo, gpost, wf1, wf2, gffpre, gffpost):
    tokens = x2.shape[0]
    rows = OUT_ROWS
    grid = (tokens // rows,)
    row_spec = pl.BlockSpec((rows, D_MODEL), lambda i: (i, 0))
    const = lambda shape: pl.BlockSpec(shape, lambda i: (0,) * len(shape),
                                       pipeline_mode=pl.Buffered(1))
    return pl.pallas_call(
        _out_kernel,
        grid=grid,
        in_specs=[
            row_spec, row_spec, row_spec, row_spec, row_spec,
            const((D_MODEL, D_MODEL)), const((D_MODEL, D_MODEL)), const((D_MODEL, D_MODEL)),
            const((1, D_MODEL)),
            const((D_MODEL, D_FF)), const((D_FF, D_MODEL)),
            const((1, D_MODEL)), const((1, D_MODEL)),
        ],
        out_specs=row_spec,
        out_shape=jax.ShapeDtypeStruct((tokens, D_MODEL), F32),
        scratch_shapes=[pltpu.VMEM((rows, D_FF), BF16)],
        compiler_params=pltpu.CompilerParams(
            dimension_semantics=("arbitrary",), vmem_limit_bytes=VMEM_LIMIT_BYTES),
        name="out",
    )(x2, a, att, ga, gb, wa, wb, wo, gpost, wf1, wf2, gffpre, gffpost)


def _rope_inv_freq_rows():
    inv_freq = ROPE_THETA ** (-jnp.arange(0, ROPE_DIM, 2, dtype=F32) / ROPE_DIM)
    return jnp.broadcast_to(inv_freq[:, None], (ROPE_HALF, LANES))


def kernel(x, positions, w_in, ln_v_gain, ln_v_bias, w_spatial, b_spatial, sinks, w_a, w_b, w_o, norm_mix_pre, norm_mix_post, w_ff_in, w_ff_out, norm_ff_pre, norm_ff_post):
    batch, seq, _ = x.shape
    depth = w_in.shape[0]
    tokens = batch * seq
    row = lambda v: v.reshape(1, -1)
    invf = _rope_inv_freq_rows()
    pos3 = positions.reshape(tokens // PROJ_ROWS, PROJ_ROWS // CHUNK, LANES)
    x2 = x.reshape(tokens, D_MODEL)
    for l in range(depth):
        u, vn, q, kd, vd, ga, gb, wa, wb, wo, wf1, wf2 = _projection(
            x2, pos3, invf, row(norm_mix_pre[l]), w_in[l],
            row(ln_v_gain[l]), row(ln_v_bias[l]),
            (w_a[l], w_b[l], w_o[l], w_ff_in[l], w_ff_out[l]))
        a, att = _mixing(sinks[l], u, vn, q, kd, vd, w_spatial[l], b_spatial[l].T,
                         seq // CHUNK)
        x2 = _output(x2, a, att, ga, gb, wa, wb, wo, row(norm_mix_post[l]), wf1, wf2,
                     row(norm_ff_pre[l]), row(norm_ff_post[l]))
    return x2.reshape(batch, seq, D_MODEL)
```

```python
import functools
import math

import jax
import jax.numpy as jnp
import numpy as np
from jax import lax
from jax.experimental import pallas as pl
from jax.experimental.pallas import tpu as pltpu

D_MODEL = 1024
SGU_GROUPS = 8
CHUNK = 128
HEAD_DIM = 64
N_Q_HEADS = 16
N_KV_HEADS = 4
Q_PER_KV = N_Q_HEADS // N_KV_HEADS
KV_WIDTH = N_KV_HEADS * HEAD_DIM
ROPE_DIM = 16
ROPE_HALF = ROPE_DIM // 2
ROPE_THETA = 500000.0
D_FF = 4 * D_MODEL
EPS = 1e-6
LANES = 128
SUBLANES = 8
NEG_INF = -1e30
KV_DUP_WIDTH = N_KV_HEADS * LANES

_OFF_U = 0
_OFF_V = _OFF_U + D_MODEL
_OFF_Q = _OFF_V + D_MODEL
_OFF_K = _OFF_Q + D_MODEL
_OFF_VA = _OFF_K + KV_WIDTH
_OFF_GA = _OFF_VA + KV_WIDTH
_OFF_GB = _OFF_GA + D_MODEL
IN_WIDTH = _OFF_GB + D_MODEL

VMEM_LIMIT_BYTES = 56 * 1024 * 1024
PROJ_ROWS = 512
MIX_BLOCKS = 8
OUT_ROWS = 512
SUB_ROWS = 256
FF_CHUNK = 1024

BF16 = jnp.bfloat16
F32 = jnp.float32


def _rms(x, gain):
    return x * lax.rsqrt(jnp.mean(x * x, axis=-1, keepdims=True) + EPS) * gain


def _dot(a, b):
    return jnp.dot(a, b, preferred_element_type=F32)


LOG2E = math.log2(math.e)


def _sigmoid(x):
    return 1.0 / (1.0 + jnp.exp2(x * -LOG2E))


_GELU_K1 = -2.0 * math.sqrt(2.0 / math.pi) * math.log2(math.e)
_GELU_K3 = _GELU_K1 * 0.044715


def _gelu_tanh(x):
    return x / (1.0 + jnp.exp2(x * (x * x * _GELU_K3 + _GELU_K1)))


def _low_half_lanes(rows):
    return lax.broadcasted_iota(jnp.int32, (rows, LANES), 1) < HEAD_DIM


def _rope_tables(pos_row, invf):
    ang = invf * pos_row
    cos = jnp.cos(ang)
    sin = jnp.sin(ang)
    pad = HEAD_DIM - ROPE_DIM
    ones = jnp.ones((pad, LANES), F32)
    zeros = jnp.zeros((pad, LANES), F32)
    cos_lt = jnp.concatenate([cos, cos, ones] * (LANES // HEAD_DIM), axis=0)
    sin_lt = jnp.concatenate([-sin, sin, zeros] * (LANES // HEAD_DIM), axis=0)
    return cos_lt.T, sin_lt.T


_W_SEGMENTS = ((_OFF_Q, D_MODEL), (_OFF_K, 2 * KV_WIDTH), (_OFF_V, D_MODEL),
               (_OFF_U, D_MODEL))
PROJ_WIDTH = _OFF_GA
GATE_PIECE = 512
_W_STAGE_SLOTS = 2
_N_PROJ_OUTS = 5
_N_PROJ_SCRATCH = 3


def _proj_kernel(x_ref, pos_ref, invf_ref, gpre_ref, w_hbm, lng_ref, lnb_ref, *refs):
    args = (x_ref, pos_ref, invf_ref, gpre_ref, w_hbm, lng_ref, lnb_ref) + refs
    pl.when(pl.program_id(0) == 0)(functools.partial(_proj_body, True, *args))
    pl.when(pl.program_id(0) != 0)(functools.partial(_proj_body, False, *args))


def _proj_body(first, x_ref, pos_ref, invf_ref, gpre_ref, w_hbm, lng_ref, lnb_ref, *refs):
    n_side = len(refs) - _N_PROJ_OUTS - _N_PROJ_SCRATCH >> 1
    side_in, refs = refs[:n_side], refs[n_side:]
    outs, refs = refs[:_N_PROJ_OUTS], refs[_N_PROJ_OUTS:]
    side_out, (w_ref, stage_ref, sem) = refs[:n_side], refs[n_side:]
    u_ref, vn_ref, q_ref, kd_ref, vd_ref = outs

    def segment_copy(k):
        lo, width = _W_SEGMENTS[k]
        slot = k % _W_STAGE_SLOTS
        return pltpu.make_async_copy(w_hbm.at[:, pl.ds(lo, width)],
                                     stage_ref.at[slot, :, pl.ds(0, width)], sem.at[slot])

    def load_segment(k):
        if not first:
            return
        lo, width = _W_SEGMENTS[k]
        segment_copy(k).wait()
        w_ref[:, lo:lo + width] = stage_ref[k % _W_STAGE_SLOTS, :, :width].astype(BF16)
        if k + _W_STAGE_SLOTS < len(_W_SEGMENTS):
            segment_copy(k + _W_STAGE_SLOTS).start()

    if first:
        for k in range(_W_STAGE_SLOTS):
            segment_copy(k).start()

    for src, dst in zip(side_in, side_out):
        dst[...] = src[...].astype(BF16)
    rows = x_ref.shape[0]
    subs = [slice(t * SUB_ROWS, (t + 1) * SUB_ROWS) for t in range(rows // SUB_ROWS)]
    hs = [_rms(x_ref[r, :], gpre_ref[...]).astype(BF16) for r in subs]

    def proj(h, lo, width):
        return _dot(h, w_ref[:, lo:lo + width])

    low = _low_half_lanes(CHUNK)
    d = lax.broadcasted_iota(jnp.int32, (CHUNK, LANES), 1) & (HEAD_DIM - 1)
    pair_next = d < ROPE_HALF
    chunks = [slice(j * CHUNK, (j + 1) * CHUNK) for j in range(rows // CHUNK)]
    tile = lambda c: slice(c * LANES, (c + 1) * LANES)
    tables = [_rope_tables(pos_ref[0, j:j + 1, :].astype(F32), invf_ref[...])
              for j in range(rows // CHUNK)]

    def rot(x, c_t, s_t):
        partner = jnp.where(pair_next, pltpu.roll(x, LANES - ROPE_HALF, axis=1),
                            pltpu.roll(x, ROPE_HALF, axis=1))
        return x * c_t + partner * s_t

    def store_dup(ref, rc, c, t):
        rolled = pltpu.roll(t, HEAD_DIM, axis=1)
        ref[rc, tile(2 * c)] = jnp.where(low, t, rolled).astype(BF16)
        ref[rc, tile(2 * c + 1)] = jnp.where(low, rolled, t).astype(BF16)

    scale = HEAD_DIM ** -0.5 * LOG2E
    load_segment(0)
    q = jnp.concatenate([proj(h, _OFF_Q, D_MODEL) for h in hs], axis=0)
    for rc, (c_tab, s_tab) in zip(chunks, tables):
        cq, sq = c_tab * scale, s_tab * scale
        for c in range(D_MODEL // LANES):
            q_ref[rc, tile(c)] = rot(q[rc, tile(c)], cq, sq).astype(BF16)
    load_segment(1)
    k = jnp.concatenate([proj(h, _OFF_K, KV_WIDTH) for h in hs], axis=0)
    for rc, (c_tab, s_tab) in zip(chunks, tables):
        for c in range(KV_WIDTH // LANES):
            store_dup(kd_ref, rc, c, rot(k[rc, tile(c)], c_tab, s_tab))

    load_segment(2)
    for r, h in zip(subs, hs):
        v = _gelu_tanh(proj(h, _OFF_V, D_MODEL))
        mu = jnp.mean(v, axis=-1, keepdims=True)
        vc = v - mu
        var = jnp.mean(vc * vc, axis=-1, keepdims=True)
        vn_ref[r, :] = (vc * lax.rsqrt(var + EPS) * lng_ref[...] + lnb_ref[...]).astype(BF16)
    load_segment(3)
    for r, h in zip(subs, hs):
        u_ref[r, :] = _gelu_tanh(proj(h, _OFF_U, D_MODEL)).astype(BF16)
    va =jnp.concatenate([proj(h, _OFF_VA, KV_WIDTH) for h in hs], axis=0)
    for rc in chunks:
        for c in range(KV_WIDTH // LANES):
            store_dup(vd_ref, rc, c, va[rc, tile(c)])


def _projection(x2, pos3, invf, gpre, w_in, lng, lnb, side_weights):
    tokens = x2.shape[0]
    rows = PROJ_ROWS
    steps = tokens // rows
    row_spec = lambda width: pl.BlockSpec((rows, width), lambda i: (i, 0))
    full = lambda shape: pl.BlockSpec(shape, lambda i: (0,) * len(shape))
    slab = lambda w: pl.BlockSpec((w.shape[0] // steps, w.shape[1]), lambda i: (i, 0))
    out_widths = (D_MODEL, D_MODEL, D_MODEL, KV_DUP_WIDTH, KV_DUP_WIDTH)
    gate_pieces = range(PROJ_WIDTH // GATE_PIECE, IN_WIDTH // GATE_PIECE)
    gate_slab = lambda p: pl.BlockSpec((D_MODEL // steps, GATE_PIECE), lambda i: (i, p))
    side_in = [slab(w) for w in side_weights] + [gate_slab(p) for p in gate_pieces]
    side_out = [slab(w) for w in side_weights] + [gate_slab(0) for _ in gate_pieces]
    side_shapes = ([jax.ShapeDtypeStruct(w.shape, BF16) for w in side_weights]
                   + [jax.ShapeDtypeStruct((D_MODEL, GATE_PIECE), BF16) for _ in gate_pieces])
    side_args = tuple(side_weights) + (w_in,) * len(gate_pieces)
    return pl.pallas_call(
        _proj_kernel,
        grid=(steps,),
        in_specs=[
            row_spec(D_MODEL),
            pl.BlockSpec((1, rows // CHUNK, LANES), lambda i: (i, 0, 0)),
            full((SUBLANES, LANES)),
            full((1, D_MODEL)),
            pl.BlockSpec(memory_space=pl.ANY),
            full((1, D_MODEL)),
            full((1, D_MODEL)),
        ] + side_in,
        out_specs=[row_spec(w) for w in out_widths] + side_out,
        out_shape=[jax.ShapeDtypeStruct((tokens, w), BF16) for w in out_widths] + side_shapes,
        scratch_shapes=[
            pltpu.VMEM((D_MODEL, PROJ_WIDTH), BF16),
            pltpu.VMEM((_W_STAGE_SLOTS, D_MODEL, D_MODEL), F32),
            pltpu.SemaphoreType.DMA((_W_STAGE_SLOTS,)),
        ],
        compiler_params=pltpu.CompilerParams(
            dimension_semantics=("arbitrary",), vmem_limit_bytes=VMEM_LIMIT_BYTES),
        name="proj",
    )(x2, pos3, invf, gpre, w_in, lng, lnb, *side_args)


def _mix_kernel(blocks_per_seq, sinks_ref, u_ref, vn_ref, q_ref, kd_ref, vd_ref,
                kdp_ref, vdp_ref, ws_ref, bs_ref, a_ref, att_ref, wtril_ref):
    i = pl.program_id(0)
    row = lax.broadcasted_iota(jnp.int32, (CHUNK, CHUNK), 0)
    col = lax.broadcasted_iota(jnp.int32, (CHUNK, CHUNK), 1)
    causal = col <= row

    @pl.when(i == 0)
    def _():
        for g in range(SGU_GROUPS):
            wtril_ref[g] = jnp.where(causal, ws_ref[g], 0.0).astype(BF16)

    blk = lambda b: slice(b * CHUNK, (b + 1) * CHUNK)
    tile = lambda c: slice(c * LANES, (c + 1) * LANES)

    def sgu_piece(g, bp):
        b0, b1 = 2 * bp, 2 * bp + 1
        rhs = jnp.concatenate([vn_ref[blk(b0), tile(g)], vn_ref[blk(b1), tile(g)]], axis=1)
        mixed = _dot(wtril_ref[g], rhs) + bs_ref[:, g:g + 1]
        a_ref[blk(b0), tile(g)] = (u_ref[blk(b0), tile(g)].astype(F32)
                                   * mixed[:, :LANES]).astype(BF16)
        a_ref[blk(b1), tile(g)] = (u_ref[blk(b1), tile(g)].astype(F32)
                                   * mixed[:, LANES:]).astype(BF16)

    first = ((i * MIX_BLOCKS) % blocks_per_seq) == 0
    prev_bias = jnp.where(first, NEG_INF, 0.0).astype(F32)
    low = col < HEAD_DIM
    ones = jnp.ones((2 * CHUNK, LANES), BF16)

    def band(ref, prev_ref, b, g):
        prev = prev_ref[:, tile(g)] if b == 0 else ref[blk(b - 1), tile(g)]
        return jnp.concatenate([prev, ref[blk(b), tile(g)]], axis=0)

    def scores(b, g):
        q_tiles = []
        for t in range(Q_PER_KV // 2):
            qt = q_ref[blk(b), tile(2 * g + t)]
            zero = jnp.zeros_like(qt)
            q_tiles += [jnp.where(low, qt, zero), jnp.where(low, zero, qt)]
        q4 = jnp.concatenate(q_tiles, axis=0)
        return lax.dot_general(q4, band(kd_ref, kdp_ref, b, g), (((1,), (1,)), ((), ())),
                               preferred_element_type=F32)

    def sink2(head):
        return sinks_ref[head] * LOG2E

    def attend(b, g, s4):
        p_rows, maxes = [], []
        for hh in range(Q_PER_KV):
            s_prev = s4[blk(hh), :CHUNK]
            if b == 0:
                s_prev = s_prev + prev_bias
            s = jnp.where(causal, s4[blk(hh), CHUNK:], s_prev)
            m = jnp.maximum(jnp.max(s, axis=-1, keepdims=True), sink2(Q_PER_KV * g + hh))
            p = jnp.exp2(s - m)
            pz = jnp.zeros_like(p)
            p_rows.append(jnp.concatenate(
                [jnp.where(causal, pz, p), jnp.where(causal, p, pz)], axis=1).astype(BF16))
            maxes.append(m)
        pband = jnp.concatenate(p_rows, axis=0)
        vband = jnp.concatenate([band(vd_ref, vdp_ref, b, g), ones], axis=1)
        o4 = _dot(pband, vband)
        for t in range(Q_PER_KV // 2):
            ha, hb = 2 * t, 2 * t + 1
            sink = jnp.where(low[:1], sink2(Q_PER_KV * g + ha), sink2(Q_PER_KV * g + hb))
            m_ab = jnp.where(low, maxes[ha], maxes[hb])
            denom = jnp.where(low, o4[blk(ha), LANES:], o4[blk(hb), LANES:]) + jnp.exp2(sink - m_ab)
            o_ab = jnp.where(low, o4[blk(ha), :LANES], o4[blk(hb), :LANES])
            att_ref[blk(b), tile(2 * g + t)] = (o_ab * (1.0 / denom)).astype(BF16)

    for g in range(SGU_GROUPS):
        for bp in range(MIX_BLOCKS // 2):
            sgu_piece(g, bp)
    for b in range(MIX_BLOCKS):
        for g in range(N_KV_HEADS):
            attend(b, g, scores(b, g))


def _mixing(sinks, u, vn, q, kd, vd, w_spatial, b_spatial_t, blocks_per_seq):
    tokens = u.shape[0]
    rows = MIX_BLOCKS * CHUNK
    grid = (tokens // rows,)
    blk = lambda width: pl.BlockSpec((rows, width), lambda i: (i, 0))
    prev = pl.BlockSpec((CHUNK, KV_DUP_WIDTH),
                        lambda i: (jnp.maximum(i * MIX_BLOCKS - 1, 0), 0))
    return pl.pallas_call(
        functools.partial(_mix_kernel, blocks_per_seq),
        grid=grid,
        in_specs=[
            pl.BlockSpec(memory_space=pltpu.SMEM),
            blk(D_MODEL), blk(D_MODEL), blk(D_MODEL), blk(KV_DUP_WIDTH), blk(KV_DUP_WIDTH),
            prev, prev,
            pl.BlockSpec((SGU_GROUPS, CHUNK, CHUNK), lambda i: (0, 0, 0)),
            pl.BlockSpec((CHUNK, SGU_GROUPS), lambda i: (0, 0)),
        ],
        out_specs=[blk(D_MODEL), blk(D_MODEL)],
        out_shape=[jax.ShapeDtypeStruct((tokens, D_MODEL), BF16)] * 2,
        scratch_shapes=[pltpu.VMEM((SGU_GROUPS, CHUNK, CHUNK), BF16)],
        compiler_params=pltpu.CompilerParams(
            dimension_semantics=("arbitrary",), vmem_limit_bytes=VMEM_LIMIT_BYTES),
        name="mix",
    )(sinks, u, vn, q, kd, vd, kd, vd, w_spatial, b_spatial_t)


_GATE_PIECES_PER_GATE = D_MODEL // GATE_PIECE


def _out_kernel(x_ref, a_ref, att_ref, gpre_ref, wa_ref, wb_ref, wo_ref,
                gpost_ref, wf1_ref, wf2_ref, gffpre_ref, gffpost_ref, *refs):
    wg_refs, (o_ref, h1_ref) = refs[:2 * _GATE_PIECES_PER_GATE], refs[2 * _GATE_PIECES_PER_GATE:]
    subs = [slice(t * SUB_ROWS, (t + 1) * SUB_ROWS) for t in range(x_ref.shape[0] // SUB_ROWS)]
    hs = [_rms(x_ref[r, :], gpre_ref[...]).astype(BF16) for r in subs]

    def gate(h, which):
        pieces = wg_refs[which * _GATE_PIECES_PER_GATE:(which + 1) * _GATE_PIECES_PER_GATE]
        return _sigmoid(jnp.concatenate([_dot(h, w[...]) for w in pieces], axis=1))

    merged = [(gate(h, 0) * _dot(a_ref[r, :], wa_ref[...])
               + gate(h, 1) * _dot(att_ref[r, :], wb_ref[...])).astype(BF16)
              for r, h in zip(subs, hs)]
    mix = [_dot(m, wo_ref[...]) for m in merged]
    x1 = [x_ref[r, :] + _rms(mx, gpost_ref[...]) for r, mx in zip(subs, mix)]
    hf = [_rms(x, gffpre_ref[...]).astype(BF16) for x in x1]
    for c in range(D_FF // FF_CHUNK):
        sl = slice(c * FF_CHUNK, (c + 1) * FF_CHUNK)
        for r, h in zip(subs, hf):
            h1 = jnp.maximum(_dot(h, wf1_ref[:, sl]), 0.0)
            h1_ref[r, sl] = (h1 * h1).astype(BF16)
    ff = [_dot(h1_ref[r, :], wf2_ref[...]) for r in subs]
    for r, x, f in zip(subs, x1, ff):
        o_ref[r, :] = x + _rms(f, gffpost_ref[...])


def _output(x2, a, att, gpre, wa, wb, wo, gpost, wf1, wf2, gffpre, gffpost, gate_weights):
    tokens = x2.shape[0]
    rows = OUT_ROWS
    grid = (tokens // rows,)
    row_spec = pl.BlockSpec((rows, D_MODEL), lambda i: (i, 0))
    const = lambda shape: pl.BlockSpec(shape, lambda i: (0,) * len(shape),
                                       pipeline_mode=pl.Buffered(1))
    return pl.pallas_call(
        _out_kernel,
        grid=grid,
        in_specs=[
            row_spec, row_spec, row_spec,
            const((1, D_MODEL)),
            const((D_MODEL, D_MODEL)), const((D_MODEL, D_MODEL)), const((D_MODEL, D_MODEL)),
            const((1, D_MODEL)),
            const((D_MODEL, D_FF)), const((D_FF, D_MODEL)),
            const((1, D_MODEL)), const((1, D_MODEL)),
        ] + [const((D_MODEL, GATE_PIECE)) for _ in gate_weights],
        out_specs=row_spec,
        out_shape=jax.ShapeDtypeStruct((tokens, D_MODEL), F32),
        scratch_shapes=[pltpu.VMEM((rows, D_FF), BF16)],
        compiler_params=pltpu.CompilerParams(
            dimension_semantics=("arbitrary",), vmem_limit_bytes=VMEM_LIMIT_BYTES),
        name="out",
    )(x2, a, att, gpre, wa, wb, wo, gpost, wf1, wf2, gffpre, gffpost, *gate_weights)


def _rope_inv_freq_rows():
    inv_freq = ROPE_THETA ** (-jnp.arange(0, ROPE_DIM, 2, dtype=F32) / ROPE_DIM)
    return jnp.broadcast_to(inv_freq[:, None], (ROPE_HALF, LANES))


def kernel(x, positions, w_in, ln_v_gain, ln_v_bias, w_spatial, b_spatial, sinks, w_a, w_b, w_o, norm_mix_pre, norm_mix_post, w_ff_in, w_ff_out, norm_ff_pre, norm_ff_post):
    batch, seq, _ = x.shape
    depth = w_in.shape[0]
    tokens = batch * seq
    row = lambda v: v.reshape(1, -1)
    invf = _rope_inv_freq_rows()
    pos3 = positions.reshape(tokens // PROJ_ROWS, PROJ_ROWS // CHUNK, LANES)
    x2 = x.reshape(tokens, D_MODEL)
    for l in range(depth):
        u, vn, q, kd, vd, wa, wb, wo, wf1, wf2, *wg = _projection(
            x2, pos3, invf, row(norm_mix_pre[l]), w_in[l],
            row(ln_v_gain[l]), row(ln_v_bias[l]),
            (w_a[l], w_b[l], w_o[l], w_ff_in[l], w_ff_out[l]))
        a, att = _mixing(sinks[l], u, vn, q, kd, vd, w_spatial[l], b_spatial[l].T,
                         seq // CHUNK)
        x2 = _output(x2, a, att, row(norm_mix_pre[l]), wa, wb, wo, row(norm_mix_post[l]),
                     wf1, wf2, row(norm_ff_pre[l]), row(norm_ff_post[l]), wg)
    return x2.reshape(batch, seq, D_MODEL)
```
